```python
import jax, jax.numpy as jnp
from jax import lax
import numpy as np

D_MODEL = 1024
BATCH = 16
SEQ = 2048
DEPTH = 4

CHUNK = 64
Q_BLOCK = 128
N_A_LAYERS = DEPTH // 2
N_B_LAYERS = DEPTH - N_A_LAYERS
D_FF = 2816
NORM_EPS = 1e-6
GDN_HEADS = 8
GDN_DK = 128
GDN_DV = 128
GDN_CONV = 4
GDN_QK = GDN_HEADS * GDN_DK
GDN_V = GDN_HEADS * GDN_DV
GDN_QKV = 2 * GDN_QK + GDN_V
GDN_IN = GDN_QKV + GDN_V + 2 * GDN_HEADS
MLA_HEADS = 8
MLA_NOPE = 128
MLA_ROPE = 64
MLA_V = 128
MLA_Q_RANK = 384
MLA_KV_RANK = 256
ROPE_THETA = 10000.0
MAX_POS_OFFSET = 65536

kernel_name = "yoco_gated_deltanet_mla_macaron"


def rms_norm(x, g):
    xf = x.astype(jnp.float32)
    y = xf * lax.rsqrt(jnp.mean(xf * xf, axis=-1, keepdims=True) + NORM_EPS)
    return (y * g.astype(jnp.float32)).astype(x.dtype)


def l2_norm(x):
    xf = x.astype(jnp.float32)
    return xf * lax.rsqrt(jnp.sum(xf * xf, axis=-1, keepdims=True) + NORM_EPS)


def swiglu(x, w_gu, w_down):
    g, u = jnp.split(x @ w_gu, 2, axis=-1)
    return (jax.nn.silu(g) * u) @ w_down


def causal_dwconv(x, w):
    K, C = w.shape
    return lax.conv_general_dilated(x, w[:, None, :].astype(x.dtype), window_strides=(1,),
                                    padding=[(K - 1, 0)],
                                    dimension_numbers=('NWC', 'WIO', 'NWC'),
                                    feature_group_count=C)


def rope(x, pos):
    half = x.shape[-1] // 2
    inv = ROPE_THETA ** (-jnp.arange(half, dtype=jnp.float32) / half)
    ang = pos.astype(jnp.float32)[..., None] * inv
    ang = ang.reshape(ang.shape[:2] + (1,) * (x.ndim - 3) + (half,))
    cos, sin = jnp.cos(ang), jnp.sin(ang)
    xf = x.astype(jnp.float32)
    x1, x2 = xf[..., :half], xf[..., half:]
    return jnp.concatenate([x1 * cos - x2 * sin, x2 * cos + x1 * sin], axis=-1).astype(x.dtype)


def gated_delta_rule(q, k, v, g, beta):
    f32 = jnp.float32
    Bn, S, H, DK = q.shape
    DV = v.shape[-1]
    N = S // CHUNK

    def chunks(t):
        t = t.astype(f32).reshape((Bn, N, CHUNK, H) + t.shape[3:])
        return jnp.moveaxis(t, (1, 3), (0, 2))

    qc, kc, vc = chunks(q), chunks(k), chunks(v)
    gc = jnp.cumsum(chunks(g), axis=-1)
    bc = chunks(beta)
    idx = jnp.arange(CHUNK)
    causal = idx[:, None] >= idx[None, :]
    strict = idx[:, None] > idx[None, :]
    gdiff = gc[..., :, None] - gc[..., None, :]
    decay = jnp.where(causal, jnp.exp(jnp.where(causal, gdiff, 0.0)), 0.0)
    k_beta = kc * bc[..., None]
    a_mat = jnp.where(strict, jnp.einsum('nbhid,nbhjd->nbhij', k_beta, kc) * decay, 0.0)
    eye = jnp.eye(CHUNK, dtype=f32)
    rhs = jnp.concatenate([vc * bc[..., None], k_beta * jnp.exp(gc)[..., None]], axis=-1)
    sol = lax.linalg.triangular_solve(a_mat + eye, rhs, left_side=True, lower=True)
    u, w = sol[..., :DV], sol[..., DV:]
    attn_intra = jnp.einsum('nbhid,nbhjd->nbhij', qc, kc) * decay
    q_dec = qc * jnp.exp(gc)[..., None]
    g_last = gc[..., -1]
    k_dec = kc * jnp.exp(g_last[..., None] - gc)[..., None]

    def step(state, inp):
        u_i, w_i, qd_i, a_i, kd_i, gl_i = inp
        v_new = u_i - jnp.einsum('bhcd,bhde->bhce', w_i, state)
        o = jnp.einsum('bhcd,bhde->bhce', qd_i, state) + jnp.einsum('bhij,bhje->bhie', a_i, v_new)
        state = state * jnp.exp(gl_i)[..., None, None] + jnp.einsum('bhcd,bhce->bhde', kd_i, v_new)
        return state, o

    s0 = jnp.zeros((Bn, H, DK, DV), f32)
    _, o = lax.scan(step, s0, (u, w, q_dec, attn_intra, k_dec, g_last))
    return jnp.moveaxis(o, (0, 2), (1, 3)).reshape(Bn, S, H, DV)


def gdn_mixer(h, w_in, conv_w, a_log, dt_bias, out_norm, w_out):
    Bn, S, _ = h.shape
    proj = h @ w_in
    qkv, gate, a, b = jnp.split(proj, [GDN_QKV, GDN_QKV + GDN_V, GDN_QKV + GDN_V + GDN_HEADS], axis=-1)
    qkv = jax.nn.silu(causal_dwconv(qkv, conv_w))
    q, k, v = jnp.split(qkv, [GDN_QK, 2 * GDN_QK], axis=-1)
    q = l2_norm(q.reshape(Bn, S, GDN_HEADS, GDN_DK)) * (GDN_DK ** -0.5)
    k = l2_norm(k.reshape(Bn, S, GDN_HEADS, GDN_DK))
    v = v.reshape(Bn, S, GDN_HEADS, GDN_DV)
    g = -jnp.exp(a_log.astype(jnp.float32)) * jax.nn.softplus(a.astype(jnp.float32) + dt_bias.astype(jnp.float32))
    beta = jax.nn.sigmoid(b.astype(jnp.float32))
    o = gated_delta_rule(q, k, v, g, beta)
    o = rms_norm(o, out_norm) * jax.nn.silu(gate.reshape(Bn, S, GDN_HEADS, GDN_DV).astype(jnp.float32))
    return o.reshape(Bn, S, GDN_V).astype(h.dtype) @ w_out


def mla_shared_kv(h, positions, kv_norm, w_kv_a, kv_a_norm, w_kv_b):
    Bn, S, _ = h.shape
    kv_a = rms_norm(h, kv_norm) @ w_kv_a
    c_kv, k_rope = kv_a[..., :MLA_KV_RANK], kv_a[..., MLA_KV_RANK:]
    c_kv = rms_norm(c_kv, kv_a_norm)
    k_rope = rope(k_rope, positions)
    kv = (c_kv @ w_kv_b).reshape(Bn, S, MLA_HEADS, MLA_NOPE + MLA_V)
    return kv[..., :MLA_NOPE], k_rope, kv[..., MLA_NOPE:]


def mla_mixer(h, positions, w_dq, q_norm, w_uq, w_o, k_nope, k_rope, v):
    Bn, S, _ = h.shape
    q = (rms_norm(h @ w_dq, q_norm) @ w_uq).reshape(Bn, S, MLA_HEADS, MLA_NOPE + MLA_ROPE)
    q_nope = q[..., :MLA_NOPE]
    q_rope = rope(q[..., MLA_NOPE:], positions)
    scale = (MLA_NOPE + MLA_ROPE) ** -0.5
    outs = []
    for blk in range(S // Q_BLOCK):
        q0, q1 = blk * Q_BLOCK, (blk + 1) * Q_BLOCK
        s = (jnp.einsum('bqhd,bkhd->bhqk', q_nope[:, q0:q1], k_nope[:, :q1])
             + jnp.einsum('bqhr,bkr->bhqk', q_rope[:, q0:q1], k_rope[:, :q1])).astype(jnp.float32) * scale
        mask = (jnp.arange(q0, q1) // CHUNK)[:, None] >= (jnp.arange(q1) // CHUNK)[None, :]
        p = jax.nn.softmax(jnp.where(mask, s, -jnp.inf), axis=-1).astype(v.dtype)
        outs.append(jnp.einsum('bhqk,bkhe->bqhe', p, v[:, :q1]))
    o = jnp.concatenate(outs, axis=1).reshape(Bn, S, MLA_HEADS * MLA_V)
    return o @ w_o


def setup_inputs(seed: int = 0) -> dict:
    key = jax.random.key(seed)
    ks = jax.random.split(key, 32)
    f32 = jnp.float32

    def nrm(k, shape, fan_in):
        return jax.random.normal(k, shape, f32) * (fan_in ** -0.5)

    def gain(k, shape):
        return 1.0 + 0.02 * jax.random.normal(k, shape, f32)

    x = jax.random.normal(ks[0], (BATCH, SEQ, D_MODEL), f32)
    offset = jax.random.randint(ks[1], (BATCH,), 0, MAX_POS_OFFSET // CHUNK, dtype=jnp.int32) * CHUNK
    positions = (offset[:, None] + jnp.arange(SEQ, dtype=jnp.int32)[None, :]).astype(jnp.int32)
    dt = jnp.exp(jax.random.uniform(ks[2], (N_A_LAYERS, GDN_HEADS), f32, jnp.log(1e-3), jnp.log(1e-1)))
    return {
        "x": x,
        "positions": positions,
        "ffn1_norm": gain(ks[3], (DEPTH, D_MODEL)),
        "ffn1_w_gu": nrm(ks[4], (DEPTH, D_MODEL, 2 * D_FF), D_MODEL),
        "ffn1_w_down": nrm(ks[5], (DEPTH, D_FF, D_MODEL), D_FF),
        "mix_norm": gain(ks[6], (DEPTH, D_MODEL)),
        "ffn2_norm": gain(ks[7], (DEPTH, D_MODEL)),
        "ffn2_w_gu": nrm(ks[8], (DEPTH, D_MODEL, 2 * D_FF), D_MODEL),
        "ffn2_w_down": nrm(ks[9], (DEPTH, D_FF, D_MODEL), D_FF),
        "gdn_w_in": nrm(ks[10], (N_A_LAYERS, D_MODEL, GDN_IN), D_MODEL),
        "gdn_conv_w": nrm(ks[11], (N_A_LAYERS, GDN_CONV, GDN_QKV), GDN_CONV),
        "gdn_a_log": jnp.log(jax.random.uniform(ks[12], (N_A_LAYERS, GDN_HEADS), f32, 1.0, 16.0)),
        "gdn_dt_bias": dt + jnp.log(-jnp.expm1(-dt)),
        "gdn_out_norm": gain(ks[13], (N_A_LAYERS, GDN_DV)),
        "gdn_w_out": nrm(ks[14], (N_A_LAYERS, GDN_V, D_MODEL), GDN_V),
        "kv_norm": gain(ks[15], (D_MODEL,)),
        "mla_w_kv_a": nrm(ks[16], (D_MODEL, MLA_KV_RANK + MLA_ROPE), D_MODEL),
        "mla_kv_a_norm": gain(ks[17], (MLA_KV_RANK,)),
        "mla_w_kv_b": nrm(ks[18], (MLA_KV_RANK, MLA_HEADS * (MLA_NOPE + MLA_V)), MLA_KV_RANK),
        "mla_w_dq": nrm(ks[19], (N_B_LAYERS, D_MODEL, MLA_Q_RANK), D_MODEL),
        "mla_q_norm": gain(ks[20], (N_B_LAYERS, MLA_Q_RANK)),
        "mla_w_uq": nrm(ks[21], (N_B_LAYERS, MLA_Q_RANK, MLA_HEADS * (MLA_NOPE + MLA_ROPE)), MLA_Q_RANK),
        "mla_w_o": nrm(ks[22], (N_B_LAYERS, MLA_HEADS * MLA_V, D_MODEL), MLA_HEADS * MLA_V),
        "final_norm": gain(ks[23], (D_MODEL,)),
    }


def reference(x, positions, ffn1_norm, ffn1_w_gu, ffn1_w_down, mix_norm, ffn2_norm, ffn2_w_gu, ffn2_w_down,
              gdn_w_in, gdn_conv_w, gdn_a_log, gdn_dt_bias, gdn_out_norm, gdn_w_out,
              kv_norm, mla_w_kv_a, mla_kv_a_norm, mla_w_kv_b,
              mla_w_dq, mla_q_norm, mla_w_uq, mla_w_o, final_norm):
    h = x
    shared = None
    for layer in range(DEPTH):
        h = h + 0.5 * swiglu(rms_norm(h, ffn1_norm[layer]), ffn1_w_gu[layer], ffn1_w_down[layer])
        hn = rms_norm(h, mix_norm[layer])
        if layer < N_A_LAYERS:
            i = layer
            h = h + gdn_mixer(hn, gdn_w_in[i], gdn_conv_w[i], gdn_a_log[i], gdn_dt_bias[i],
                              gdn_out_norm[i], gdn_w_out[i])
        else:
            j = layer - N_A_LAYERS
            k_nope, k_rope, v = shared
            h = h + mla_mixer(hn, positions, mla_w_dq[j], mla_q_norm[j], mla_w_uq[j], mla_w_o[j],
                              k_nope, k_rope, v)
        h = h + 0.5 * swiglu(rms_norm(h, ffn2_norm[layer]), ffn2_w_gu[layer], ffn2_w_down[layer])
        if layer == N_A_LAYERS - 1:
            shared = mla_shared_kv(h, positions, kv_norm, mla_w_kv_a, mla_kv_a_norm, mla_w_kv_b)
    return rms_norm(h, final_norm)
```

```python
import functools

import jax
import jax.numpy as jnp
import numpy as np
from jax import lax
from jax.experimental import pallas as pl
from jax.experimental.pallas import tpu as pltpu

F32 = jnp.float32
BF16 = jnp.bfloat16

NORM_EPS = 1e-6
CHUNK = 64
GDN_HEADS = 8
GDN_DK = 128
GDN_CONV = 4
MLA_HEADS = 8
MLA_NOPE = 128
MLA_ROPE = 64
MLA_V = 128
MLA_KV_RANK = 256
ROPE_THETA = 10000.0
LANES = 128
SUBLANES = 8
INV_BLOCK = 16
VMEM_LIMIT = 56 * 1024 * 1024


def _resident(shape):
    nd = len(shape)
    return pl.BlockSpec(shape, lambda *_: (0,) * nd, pipeline_mode=pl.Buffered(1))


def _params(sem):
    return pltpu.CompilerParams(dimension_semantics=sem, vmem_limit_bytes=VMEM_LIMIT)


def _rms(x, g):
    ms = jnp.mean(x * x, axis=-1, keepdims=True)
    return x * lax.rsqrt(ms + NORM_EPS) * g


def _silu(x):
    return x * jax.nn.sigmoid(x)


def _dot(a, b):
    return jnp.dot(a, b, preferred_element_type=F32)


def _dot_nt(a, b):
    return lax.dot_general(a, b, (((1,), (1,)), ((), ())), preferred_element_type=F32)


def _dot_tn(a, b):
    return lax.dot_general(a, b, (((0,), (0,)), ((), ())), preferred_element_type=F32)


def _dot_hi(a, b):
    return jnp.dot(a, b, preferred_element_type=F32, precision=lax.Precision.HIGHEST)


def _ffn_kernel(x_ref, g_ref, wgu_ref, wd_ref, *rest, d_ff, tf, final):
    if final:
        gf_ref, o_ref, xn_ref, act_ref = rest
    else:
        o_ref, xn_ref, act_ref = rest
    xn_ref[...] = _rms(x_ref[...], g_ref[...]).astype(BF16)
    for c in range(d_ff // tf):
        g = _dot(xn_ref[...], wgu_ref[:, c * tf:(c + 1) * tf])
        u = _dot(xn_ref[...], wgu_ref[:, d_ff + c * tf:d_ff + (c + 1) * tf])
        act_ref[:, c * tf:(c + 1) * tf] = (_silu(g) * u).astype(BF16)
    y = x_ref[...] + 0.5 * _dot(act_ref[...], wd_ref[...])
    if final:
        y = _rms(y, gf_ref[...])
    o_ref[...] = y


def _ffn(h, g, wgu, wd, final_g=None, tm=512, tf=256):
    m, d = h.shape
    d_ff = wd.shape[0]
    final = final_g is not None
    row = pl.BlockSpec((tm, d), lambda i: (i, 0))
    in_specs = [row, _resident((1, d)), _resident(wgu.shape), _resident(wd.shape)]
    args = [h, g.reshape(1, d), wgu, wd]
    if final:
        in_specs.append(_resident((1, d)))
        args.append(final_g.reshape(1, d))
    return pl.pallas_call(
        functools.partial(_ffn_kernel, d_ff=d_ff, tf=tf, final=final),
        grid=(m // tm,),
        in_specs=in_specs,
        out_specs=row,
        out_shape=jax.ShapeDtypeStruct((m, d), F32),
        scratch_shapes=[pltpu.VMEM((tm, d), BF16), pltpu.VMEM((tm, d_ff), BF16)],
        compiler_params=_params(("parallel",)),
        name="ffn_final" if final else "ffn",
    )(*args)


def _proj_res_kernel(a_ref, w_ref, r_ref, o_ref):
    o_ref[...] = r_ref[...] + _dot(a_ref[...], w_ref[...])


def _proj_res(a, w, res, tm=512):
    m, k = a.shape
    n = w.shape[1]
    return pl.pallas_call(
        _proj_res_kernel,
        grid=(m // tm,),
        in_specs=[pl.BlockSpec((tm, k), lambda i: (i, 0)), _resident(w.shape),
                  pl.BlockSpec((tm, n), lambda i: (i, 0))],
        out_specs=pl.BlockSpec((tm, n), lambda i: (i, 0)),
        out_shape=jax.ShapeDtypeStruct((m, n), F32),
        compiler_params=_params(("parallel",)),
        name="proj_res",
    )(a, w, res)


def _gdn_in_kernel(x_ref, g_ref, wqkv_ref, wgate_ref, wab_ref, cw_ref, alog_ref, dtb_ref,
                   q_ref, k_ref, v_ref, gate_ref, gb_ref, xn_ref, pbuf_ref,
                   *, tm, tiles_per_seq, d_qk):
    i = pl.program_id(0)
    xn_ref[...] = _rms(x_ref[...], g_ref[...]).astype(BF16)

    @pl.when(i % tiles_per_seq == 0)
    def _():
        pbuf_ref[0:SUBLANES, :] = jnp.zeros((SUBLANES, pbuf_ref.shape[1]), F32)

    outs = (q_ref, k_ref, v_ref)
    for part in range(3):
        cols = slice(part * d_qk, (part + 1) * d_qk)
        pbuf_ref[SUBLANES:SUBLANES + tm, cols] = _dot(xn_ref[...], wqkv_ref[:, cols])
        y = cw_ref[GDN_CONV - 1:GDN_CONV, cols] * pbuf_ref[SUBLANES:SUBLANES + tm, cols]
        for j in range(GDN_CONV - 1):
            off = SUBLANES - (GDN_CONV - 1) + j
            y = y + cw_ref[j:j + 1, cols] * pbuf_ref[off:off + tm, cols]
        y = _silu(y)
        if part < 2:
            scale = GDN_DK ** -0.5 if part == 0 else 1.0
            for h in range(GDN_HEADS):
                hc = slice(h * GDN_DK, (h + 1) * GDN_DK)
                blk = y[:, hc]
                ss = jnp.sum(blk * blk, axis=-1, keepdims=True)
                outs[part][:, hc] = (blk * (lax.rsqrt(ss + NORM_EPS) * scale)).astype(BF16)
        else:
            v_ref[...] = y.astype(BF16)
    pbuf_ref[0:SUBLANES, :] = pbuf_ref[tm:tm + SUBLANES, :]

    gate_ref[...] = _dot(xn_ref[...], wgate_ref[...])

    ab = _dot(xn_ref[...], wab_ref[...])
    x = ab + dtb_ref[...]
    softplus = jnp.maximum(x, 0.0) + jnp.log1p(jnp.exp(-jnp.abs(x)))
    decay = -jnp.exp(alog_ref[...]) * softplus
    lane = lax.broadcasted_iota(jnp.int32, ab.shape, 1)
    gb_ref[...] = jnp.where(lane < GDN_HEADS, decay, jax.nn.sigmoid(ab))


def _gdn_in(h, g, wqkv, wgate, wab, conv_w, alog_row, dtb_row, seq, tm=512):
    m, d = h.shape
    d_qk = wgate.shape[1]
    row = lambda n: pl.BlockSpec((tm, n), lambda i: (i, 0))
    return pl.pallas_call(
        functools.partial(_gdn_in_kernel, tm=tm, tiles_per_seq=seq // tm, d_qk=d_qk),
        grid=(m // tm,),
        in_specs=[row(d), _resident((1, d)), _resident(wqkv.shape), _resident(wgate.shape),
                  _resident(wab.shape), _resident(conv_w.shape), _resident((1, LANES)),
                  _resident((1, LANES))],
        out_specs=[row(d_qk), row(d_qk), row(d_qk), row(d_qk), row(LANES)],
        out_shape=[jax.ShapeDtypeStruct((m, d_qk), BF16)] * 3
        + [jax.ShapeDtypeStruct((m, d_qk), F32), jax.ShapeDtypeStruct((m, LANES), F32)],
        scratch_shapes=[pltpu.VMEM((tm, d), BF16), pltpu.VMEM((tm + SUBLANES, 3 * d_qk), F32)],
        compiler_params=_params(("arbitrary",)),
        name="gdn_in",
    )(h, g.reshape(1, d), wqkv, wgate, wab, conv_w, alog_row, dtb_row)


def _split3(x):
    hi = x.astype(BF16)
    r = x - hi.astype(F32)
    mid = r.astype(BF16)
    lo = (r - mid.astype(F32)).astype(BF16)
    return hi, mid, lo


def _delta_kernel(q_ref, k_ref, v_ref, gate_ref, gb_ref, on_ref, o_ref, state_ref, *, tc):
    @pl.when(pl.program_id(1) == 0)
    def _():
        state_ref[...] = jnp.zeros(state_ref.shape, F32)

    ri = lax.broadcasted_iota(jnp.int32, (CHUNK, CHUNK), 0)
    ci = lax.broadcasted_iota(jnp.int32, (CHUNK, CHUNK), 1)
    causal = ri >= ci
    strict = ri > ci
    same_blk = (ri // INV_BLOCK) == (ci // INV_BLOCK)
    eye = jnp.where(ri == ci, 1.0, 0.0).astype(F32)
    tril = jnp.where(causal, 1.0, 0.0).astype(BF16)

    def chunk(c, carry):
        rows = pl.ds(pl.multiple_of(c * CHUNK, CHUNK), CHUNK)
        gb = gb_ref[rows, :]
        hi, mid, lo = _split3(gb)
        gcum = _dot(tril, hi) + _dot(tril, mid) + _dot(tril, lo)
        gcum_t = gcum.T
        for h in range(GDN_HEADS):
            hc = slice(h * GDN_DK, (h + 1) * GDN_DK)
            q = q_ref[rows, hc]
            k = k_ref[rows, hc]
            v = v_ref[rows, hc].astype(F32)
            kf = k.astype(F32)
            gc_col = gcum[:, h:h + 1]
            gc_row = gcum_t[h:h + 1, :]
            g_last = gcum[CHUNK - 1:CHUNK, h:h + 1]
            beta = gb[:, GDN_HEADS + h:GDN_HEADS + h + 1]
            decay = jnp.where(causal, jnp.exp(jnp.where(causal, gc_col - gc_row, 0.0)), 0.0)
            eg = jnp.exp(gc_col)
            kb = kf * beta
            kq = _dot_nt(jnp.concatenate([kb.astype(BF16), q], axis=0), k)
            a_mat = jnp.where(strict, kq[:CHUNK] * decay, 0.0)
            attn = kq[CHUNK:] * decay

            ad = jnp.where(same_blk, a_mat, 0.0)
            ao = a_mat - ad
            ad2 = _dot_hi(ad, ad)
            ad3 = _dot_hi(ad, ad2)
            ad4 = _dot_hi(ad2, ad2)
            ad8 = _dot_hi(ad4, ad4)
            p = eye - ad + ad2 - ad3
            p = p + _dot_hi(p, ad4)
            xd = p + _dot_hi(p, ad8)
            b = _dot_hi(xd, ao)
            b2 = _dot_hi(b, b)
            rhs = jnp.concatenate([v * beta, kb * eg], axis=1)
            y = _dot_hi(xd, rhs)
            y = y + _dot_hi(b2, y)
            y = y - _dot_hi(b, y)
            u = y[:, :GDN_DK]
            w = y[:, GDN_DK:]

            s = state_ref[h]
            qd = q.astype(F32) * eg
            ws_qs = _dot(jnp.concatenate([w, qd], axis=0).astype(BF16), s.astype(BF16))
            v_new = u - ws_qs[:CHUNK]
            o = ws_qs[CHUNK:] + _dot(attn.astype(BF16), v_new.astype(BF16))
            kd = kf * jnp.exp(g_last - gc_col)
            state_ref[h] = s * jnp.exp(g_last) + _dot_tn(kd.astype(BF16), v_new.astype(BF16))

            on = _rms(o, on_ref[...]) * _silu(gate_ref[rows, hc])
            o_ref[rows, hc] = on.astype(BF16)
        return carry

    lax.fori_loop(0, tc // CHUNK, chunk, 0)


def _delta(q, k, v, gate, gb, out_norm, batch, seq, tc=512):
    m, d = q.shape
    nt = seq // tc
    row = lambda n: pl.BlockSpec((tc, n), lambda b, t: (b * nt + t, 0))
    return pl.pallas_call(
        functools.partial(_delta_kernel, tc=tc),
        grid=(batch, nt),
        in_specs=[row(d), row(d), row(d), row(d), row(LANES), _resident((1, GDN_DK))],
        out_specs=row(d),
        out_shape=jax.ShapeDtypeStruct((m, d), BF16),
        scratch_shapes=[pltpu.VMEM((GDN_HEADS, GDN_DK, GDN_DK), F32)],
        compiler_params=_params(("parallel", "arbitrary")),
        name="gdn_delta",
    )(q, k, v, gate, gb, out_norm.reshape(1, GDN_DK))


def _rope_angles(pos_ref, inv_ref):
    ang = pos_ref[...].astype(F32) * inv_ref[...]
    return jnp.cos(ang), jnp.sin(ang)


def _mla_kv_kernel(x_ref, g_ref, wc_ref, wr_ref, wrr_ref, g2_ref, wkb_ref, wvb_ref,
                   pos_ref, inv_ref, kn_ref, kr_ref, v_ref):
    xn = _rms(x_ref[...], g_ref[...]).astype(BF16)
    cn = _rms(_dot(xn, wc_ref[...]), g2_ref[...]).astype(BF16)
    cos, sin = _rope_angles(pos_ref, inv_ref)
    kr_ref[...] = (_dot(xn, wr_ref[...]) * cos + _dot(xn, wrr_ref[...]) * sin).astype(BF16)
    kn_ref[...] = _dot(cn, wkb_ref[...]).astype(BF16)
    v_ref[...] = _dot(cn, wvb_ref[...]).astype(BF16)


def _mla_kv(h, g, wc, wr, wrr, g2, wkb, wvb, pos, inv, tm=512):
    m, d = h.shape
    n = wkb.shape[1]
    row = lambda w: pl.BlockSpec((tm, w), lambda i: (i, 0))
    return pl.pallas_call(
        _mla_kv_kernel,
        grid=(m // tm,),
        in_specs=[row(d), _resident((1, d)), _resident(wc.shape), _resident(wr.shape),
                  _resident(wrr.shape), _resident((1, wc.shape[1])), _resident(wkb.shape),
                  _resident(wvb.shape), row(1), _resident((1, LANES))],
        out_specs=[row(n), row(LANES), row(n)],
        out_shape=[jax.ShapeDtypeStruct((m, n), BF16), jax.ShapeDtypeStruct((m, LANES), BF16),
                   jax.ShapeDtypeStruct((m, n), BF16)],
        compiler_params=_params(("parallel",)),
        name="mla_kv",
    )(h, g.reshape(1, d), wc, wr, wrr, g2.reshape(1, -1), wkb, wvb, pos, inv)


def _mla_q_kernel(x_ref, g_ref, wdq_ref, g2_ref, wqn_ref, wqr_ref, wqrr_ref, pos_ref, inv_ref,
                  qn_ref, qr_ref, *, scale):
    xn = _rms(x_ref[...], g_ref[...]).astype(BF16)
    ql = _rms(_dot(xn, wdq_ref[...]), g2_ref[...]).astype(BF16)
    qn_ref[...] = (_dot(ql, wqn_ref[...]) * scale).astype(BF16)
    cos, sin = _rope_angles(pos_ref, inv_ref)
    cos = cos * scale
    sin = sin * scale
    qr = _dot(ql, wqr_ref[...])
    qrr = _dot(ql, wqrr_ref[...])
    for h in range(MLA_HEADS):
        hc = slice(h * LANES, (h + 1) * LANES)
        qr_ref[:, hc] = (qr[:, hc] * cos + qrr[:, hc] * sin).astype(BF16)


def _mla_q(h, g, wdq, g2, wqn, wqr, wqrr, pos, inv, scale, tm=512):
    m, d = h.shape
    n = wqn.shape[1]
    row = lambda w: pl.BlockSpec((tm, w), lambda i: (i, 0))
    return pl.pallas_call(
        functools.partial(_mla_q_kernel, scale=scale),
        grid=(m // tm,),
        in_specs=[row(d), _resident((1, d)), _resident(wdq.shape), _resident((1, wdq.shape[1])),
                  _resident(wqn.shape), _resident(wqr.shape), _resident(wqrr.shape), row(1),
                  _resident((1, LANES))],
        out_specs=[row(n), row(n)],
        out_shape=[jax.ShapeDtypeStruct((m, n), BF16)] * 2,
        compiler_params=_params(("parallel",)),
        name="mla_q",
    )(h, g.reshape(1, d), wdq, g2.reshape(1, -1), wqn, wqr, wqrr, pos, inv)


def _attn_kernel(qn_ref, qr_ref, kn_ref, kr_ref, v_ref, o_ref, *, t):
    qi = pl.program_id(1)
    ri = lax.broadcasted_iota(jnp.int32, (t, t), 0) // CHUNK
    ci = lax.broadcasted_iota(jnp.int32, (t, t), 1) // CHUNK
    diag_mask = ri >= ci

    for h in range(MLA_HEADS):
        hc = slice(h * LANES, (h + 1) * LANES)
        q = jnp.concatenate([qn_ref[:, hc], qr_ref[:, hc]], axis=1)

        def step(j, carry, masked):
            m, l, acc = carry
            rows = pl.ds(pl.multiple_of(j * t, t), t)
            kc = jnp.concatenate([kn_ref[rows, hc], kr_ref[rows, :]], axis=1)
            s = _dot_nt(q, kc)
            if masked:
                s = jnp.where(diag_mask, s, -jnp.inf)
            m_new = jnp.maximum(m, jnp.max(s, axis=-1, keepdims=True))
            alpha = jnp.exp(m - m_new)
            p = jnp.exp(s - m_new)
            l = alpha * l + jnp.sum(p, axis=-1, keepdims=True)
            acc = alpha * acc + _dot(p.astype(BF16), v_ref[rows, hc])
            return m_new, l, acc

        init = (jnp.full((t, 1), -jnp.inf, F32), jnp.zeros((t, 1), F32), jnp.zeros((t, MLA_V), F32))
        carry = lax.fori_loop(0, qi, functools.partial(step, masked=False), init)
        _, l, acc = step(qi, carry, True)
        o_ref[:, hc] = (acc / l).astype(BF16)


def _attn(qn, qr, kn, kr, v, batch, seq, t=256):
    m, d = qn.shape
    nq = seq // t
    qspec = pl.BlockSpec((t, d), lambda b, i: (b * nq + i, 0))
    kspec = lambda w: pl.BlockSpec((seq, w), lambda b, i: (b, 0))
    return pl.pallas_call(
        functools.partial(_attn_kernel, t=t),
        grid=(batch, nq),
        in_specs=[qspec, qspec, kspec(d), kspec(LANES), kspec(d)],
        out_specs=qspec,
        out_shape=jax.ShapeDtypeStruct((m, d), BF16),
        compiler_params=_params(("parallel", "arbitrary")),
        name="mla_attn",
    )(qn, qr, kn, kr, v)


def _rot_half_cols(w, heads):
    k = w.shape[0]
    w = w.reshape(k, heads, 2, MLA_ROPE // 2)
    return jnp.stack([-w[:, :, 1], w[:, :, 0]], axis=2).reshape(k, heads * MLA_ROPE)


def _pad_heads(w, heads):
    k = w.shape[0]
    w = w.reshape(k, heads, MLA_ROPE)
    return jnp.pad(w, ((0, 0), (0, 0), (0, LANES - MLA_ROPE))).reshape(k, heads * LANES)


def kernel(x, positions, ffn1_norm, ffn1_w_gu, ffn1_w_down, mix_norm, ffn2_norm, ffn2_w_gu, ffn2_w_down, gdn_w_in, gdn_conv_w, gdn_a_log, gdn_dt_bias, gdn_out_norm, gdn_w_out, kv_norm, mla_w_kv_a, mla_kv_a_norm, mla_w_kv_b, mla_w_dq, mla_q_norm, mla_w_uq, mla_w_o, final_norm):
    batch, seq, d = x.shape
    m = batch * seq
    depth = ffn1_norm.shape[0]
    n_a = gdn_w_in.shape[0]
    d_qk = GDN_HEADS * GDN_DK
    bf = lambda w: w.astype(BF16)

    h = x.reshape(m, d)
    pos = positions.reshape(m, 1)
    half = MLA_ROPE // 2
    inv = ROPE_THETA ** (-jnp.arange(half, dtype=F32) / half)
    inv = jnp.pad(jnp.concatenate([inv, inv]), (0, LANES - MLA_ROPE)).reshape(1, LANES)
    scale = (MLA_NOPE + MLA_ROPE) ** -0.5

    kn = kr = vv = None
    for layer in range(depth):
        h = _ffn(h, ffn1_norm[layer], bf(ffn1_w_gu[layer]), bf(ffn1_w_down[layer]))
        if layer < n_a:
            i = layer
            w_in = gdn_w_in[i]
            wab = jnp.pad(w_in[:, 4 * d_qk:], ((0, 0), (0, LANES - 2 * GDN_HEADS)))
            pad_row = lambda p: jnp.pad(p.astype(F32), (0, LANES - GDN_HEADS)).reshape(1, LANES)
            q, k, v, gate, gb = _gdn_in(
                h, mix_norm[layer], bf(w_in[:, :3 * d_qk]), bf(w_in[:, 3 * d_qk:4 * d_qk]), bf(wab),
                gdn_conv_w[i], pad_row(gdn_a_log[i]), pad_row(gdn_dt_bias[i]), seq)
            y = _delta(q, k, v, gate, gb, gdn_out_norm[i], batch, seq)
            h = _proj_res(y, bf(gdn_w_out[i]), h)
        else:
            j = layer - n_a
            w_uq = mla_w_uq[j].reshape(-1, MLA_HEADS, MLA_NOPE + MLA_ROPE)
            wqn = w_uq[:, :, :MLA_NOPE].reshape(-1, MLA_HEADS * MLA_NOPE)
            wqr = w_uq[:, :, MLA_NOPE:].reshape(-1, MLA_HEADS * MLA_ROPE)
            qn, qr = _mla_q(h, mix_norm[layer], bf(mla_w_dq[j]), mla_q_norm[j], bf(wqn),
                            bf(_pad_heads(wqr, MLA_HEADS)),
                            bf(_pad_heads(_rot_half_cols(wqr, MLA_HEADS), MLA_HEADS)),
                            pos, inv, scale)
            o = _attn(qn, qr, kn, kr, vv, batch, seq)
            h = _proj_res(o, bf(mla_w_o[j]), h)
        last = layer == depth - 1
        h = _ffn(h, ffn2_norm[layer], bf(ffn2_w_gu[layer]), bf(ffn2_w_down[layer]),
                 final_g=final_norm if last else None)
        if layer == n_a - 1:
            wr = mla_w_kv_a[:, MLA_KV_RANK:]
            w_kv_b = mla_w_kv_b.reshape(MLA_KV_RANK, MLA_HEADS, MLA_NOPE + MLA_V)
            kn, kr, vv = _mla_kv(
                h, kv_norm, bf(mla_w_kv_a[:, :MLA_KV_RANK]), bf(_pad_heads(wr, 1)),
                bf(_pad_heads(_rot_half_cols(wr, 1), 1)), mla_kv_a_norm,
                bf(w_kv_b[:, :, :MLA_NOPE].reshape(MLA_KV_RANK, -1)),
                bf(w_kv_b[:, :, MLA_NOPE:].reshape(MLA_KV_RANK, -1)), pos, inv)
    return h.reshape(batch, seq, d)
```

```python
import functools

import jax
import jax.numpy as jnp
import numpy as np
from jax import lax
from jax.experimental import pallas as pl
from jax.experimental.pallas import tpu as pltpu

F32 = jnp.float32
BF16 = jnp.bfloat16

NORM_EPS = 1e-6
CHUNK = 64
GDN_HEADS = 8
GDN_PAIRS = GDN_HEADS // 2
GDN_DK = 128
GDN_CONV = 4
MLA_HEADS = 8
MLA_NOPE = 128
MLA_ROPE = 64
MLA_V = 128
MLA_KV_RANK = 256
ROPE_THETA = 10000.0
LANES = 128
SUBLANES = 8
INV_BLOCK = 16
VMEM_LIMIT = 56 * 1024 * 1024


def _resident(shape):
    nd = len(shape)
    return pl.BlockSpec(shape, lambda *_: (0,) * nd, pipeline_mode=pl.Buffered(1))


def _params(sem):
    return pltpu.CompilerParams(dimension_semantics=sem, vmem_limit_bytes=VMEM_LIMIT)


def _rms(x, g):
    ms = jnp.mean(x * x, axis=-1, keepdims=True)
    return x * lax.rsqrt(ms + NORM_EPS) * g


def _silu(x):
    return x * jax.nn.sigmoid(x)


def _dot(a, b):
    return jnp.dot(a, b, preferred_element_type=F32)


def _dot_nt(a, b):
    return lax.dot_general(a, b, (((1,), (1,)), ((), ())), preferred_element_type=F32)


def _cat0(*xs):
    return jnp.concatenate(xs, axis=0)


def _cat1(*xs):
    return jnp.concatenate(xs, axis=1)


def _ffn_kernel(*refs, d_ff, tf, proj, final):
    refs = list(refs)
    x_ref = refs.pop(0)
    if proj:
        a_ref, wp_ref = refs.pop(0), refs.pop(0)
    g_ref, wgu_ref, wd_ref = refs.pop(0), refs.pop(0), refs.pop(0)
    if final:
        gf_ref = refs.pop(0)
    o_ref, xn_ref, act_ref = refs
    if proj:
        o_ref[...] = x_ref[...] + _dot(a_ref[...], wp_ref[...])
        h_ref = o_ref
    else:
        h_ref = x_ref
    xn_ref[...] = _rms(h_ref[...], g_ref[...]).astype(BF16)
    for c in range(d_ff // tf):
        g = _dot(xn_ref[...], wgu_ref[:, c * tf:(c + 1) * tf])
        u = _dot(xn_ref[...], wgu_ref[:, d_ff + c * tf:d_ff + (c + 1) * tf])
        act_ref[:, c * tf:(c + 1) * tf] = (_silu(g) * u).astype(BF16)
    y = h_ref[...] + 0.5 * _dot(act_ref[...], wd_ref[...])
    if final:
        y = _rms(y, gf_ref[...])
    o_ref[...] = y


def _ffn(h, g, wgu, wd, proj=None, final_g=None, tm=512, tf=256):
    m, d = h.shape
    d_ff = wd.shape[0]
    row = lambda n: pl.BlockSpec((tm, n), lambda i: (i, 0))
    in_specs, args = [row(d)], [h]
    if proj is not None:
        a, wp = proj
        in_specs += [row(a.shape[1]), _resident(wp.shape)]
        args += [a, wp]
    in_specs += [_resident((1, d)), _resident(wgu.shape), _resident(wd.shape)]
    args += [g.reshape(1, d), wgu, wd]
    if final_g is not None:
        in_specs.append(_resident((1, d)))
        args.append(final_g.reshape(1, d))
    return pl.pallas_call(
        functools.partial(_ffn_kernel, d_ff=d_ff, tf=tf, proj=proj is not None,
                          final=final_g is not None),
        grid=(m // tm,),
        in_specs=in_specs,
        out_specs=row(d),
        out_shape=jax.ShapeDtypeStruct((m, d), F32),
        scratch_shapes=[pltpu.VMEM((tm, d), BF16), pltpu.VMEM((tm, d_ff), BF16)],
        compiler_params=_params(("parallel",)),
        name="ffn" + ("_proj" if proj is not None else "") + ("_final" if final_g is not None else ""),
    )(*args)


def _gdn_in_kernel(x_ref, g_ref, wqkv_ref, wgate_ref, wab_ref, cw_ref, alog_ref, dtb_ref,
                   q_ref, k_ref, v_ref, gate_ref, gb_ref, xn_ref, pbuf_ref,
                   *, tm, tiles_per_seq, d_qk):
    i = pl.program_id(0)
    xn_ref[...] = _rms(x_ref[...], g_ref[...]).astype(BF16)

    @pl.when(i % tiles_per_seq == 0)
    def _():
        pbuf_ref[0:SUBLANES, :] = jnp.zeros((SUBLANES, pbuf_ref.shape[1]), F32)

    outs = (q_ref, k_ref, v_ref)
    for part in range(3):
        cols = slice(part * d_qk, (part + 1) * d_qk)
        pbuf_ref[SUBLANES:SUBLANES + tm, cols] = _dot(xn_ref[...], wqkv_ref[:, cols])
        y = cw_ref[GDN_CONV - 1:GDN_CONV, cols] * pbuf_ref[SUBLANES:SUBLANES + tm, cols]
        for j in range(GDN_CONV - 1):
            off = SUBLANES - (GDN_CONV - 1) + j
            y = y + cw_ref[j:j + 1, cols] * pbuf_ref[off:off + tm, cols]
        y = _silu(y)
        if part < 2:
            scale = GDN_DK ** -0.5 if part == 0 else 1.0
            for h in range(GDN_HEADS):
                hc = slice(h * GDN_DK, (h + 1) * GDN_DK)
                blk = y[:, hc]
                ss = jnp.sum(blk * blk, axis=-1, keepdims=True)
                outs[part][:, hc] = (blk * (lax.rsqrt(ss + NORM_EPS) * scale)).astype(BF16)
        else:
            v_ref[...] = y.astype(BF16)
    pbuf_ref[0:SUBLANES, :] = pbuf_ref[tm:tm + SUBLANES, :]

    gate_ref[...] = _dot(xn_ref[...], wgate_ref[...])

    ab = _dot(xn_ref[...], wab_ref[...])
    x = ab + dtb_ref[...]
    softplus = jnp.maximum(x, 0.0) + jnp.log1p(jnp.exp(-jnp.abs(x)))
    decay = -jnp.exp(alog_ref[...]) * softplus
    lane = lax.broadcasted_iota(jnp.int32, ab.shape, 1)
    gb_ref[...] = jnp.where(lane < GDN_HEADS, decay, jax.nn.sigmoid(ab))


def _gdn_in(h, g, wqkv, wgate, wab, conv_w, alog_row, dtb_row, seq, tm=512):
    m, d = h.shape
    d_qk = wgate.shape[1]
    row = lambda n: pl.BlockSpec((tm, n), lambda i: (i, 0))
    return pl.pallas_call(
        functools.partial(_gdn_in_kernel, tm=tm, tiles_per_seq=seq // tm, d_qk=d_qk),
        grid=(m // tm,),
        in_specs=[row(d), _resident((1, d)), _resident(wqkv.shape), _resident(wgate.shape),
                  _resident(wab.shape), _resident(conv_w.shape), _resident((1, LANES)),
                  _resident((1, LANES))],
        out_specs=[row(d_qk), row(d_qk), row(d_qk), row(d_qk), row(LANES)],
        out_shape=[jax.ShapeDtypeStruct((m, d_qk), BF16)] * 3
        + [jax.ShapeDtypeStruct((m, d_qk), F32), jax.ShapeDtypeStruct((m, LANES), F32)],
        scratch_shapes=[pltpu.VMEM((tm, d), BF16), pltpu.VMEM((tm + SUBLANES, 3 * d_qk), F32)],
        compiler_params=_params(("arbitrary",)),
        name="gdn_in",
    )(h, g.reshape(1, d), wqkv, wgate, wab, conv_w, alog_row, dtb_row)


def _split2(x):
    hi = x.astype(BF16)
    mid = (x - hi.astype(F32)).astype(BF16)
    return hi, mid


def _split3(x):
    hi = x.astype(BF16)
    r = x - hi.astype(F32)
    mid = r.astype(BF16)
    lo = (r - mid.astype(F32)).astype(BF16)
    return hi, mid, lo


def _dot_split2(x, w):
    hi, mid = _split2(x)
    r = _dot(_cat0(hi, mid), w)
    n = x.shape[0]
    return r[:n] + r[n:]


def _delta_selectors():
    nb = INV_BLOCK
    pack = np.zeros((GDN_PAIRS * LANES, LANES), np.float32)
    for r in range(GDN_PAIRS * LANES):
        p, l = divmod(r, LANES)
        h, c = 2 * p + l // CHUNK, l % CHUNK
        pack[r, nb * h + c % nb] = 1.0
    bcast = np.zeros((LANES, (nb - 1) * LANES), np.float32)
    for j in range(nb - 1):
        for c in range(LANES):
            bcast[nb * (c // nb) + j, LANES * j + c] = 1.0
    unpack = np.zeros((LANES, GDN_HEADS * CHUNK), np.float32)
    for h in range(GDN_HEADS):
        for c in range(CHUNK):
            unpack[nb * h + c % nb, CHUNK * h + c] = 1.0
    return tuple(jnp.asarray(a, BF16) for a in (pack, bcast, unpack))


def _delta_kernel(q_ref, k_ref, v_ref, gate_ref, gb_ref, on_ref, pack_ref, bcast_ref, unpack_ref,
                  o_ref, state_ref, *, tc, ns):
    @pl.when(pl.program_id(1) == 0)
    def _():
        state_ref[...] = jnp.zeros(state_ref.shape, F32)

    nb = INV_BLOCK
    ri = lax.broadcasted_iota(jnp.int32, (CHUNK, LANES), 0)
    li = lax.broadcasted_iota(jnp.int32, (CHUNK, LANES), 1)
    ci = li % CHUNK
    lo64 = li < CHUNK
    lo64_row = lo64[0:1]
    causal = ri >= ci
    strict = ri > ci
    bd16 = (ri // nb) == (ci // nb)
    bd32 = (ri // (2 * nb)) == (ci // (2 * nb))
    off32 = jnp.logical_and(bd32, jnp.logical_not(bd16))
    off64 = jnp.logical_not(bd32)
    lo128 = lax.broadcasted_iota(jnp.int32, (LANES, LANES), 1) < CHUNK
    tril = jnp.where(lax.broadcasted_iota(jnp.int32, (CHUNK, CHUNK), 0)
                     >= lax.broadcasted_iota(jnp.int32, (CHUNK, CHUNK), 1), 1.0, 0.0).astype(BF16)
    sub_i = lax.broadcasted_iota(jnp.int32, (SUBLANES, LANES), 0)
    sub_l = lax.broadcasted_iota(jnp.int32, (SUBLANES, LANES), 1) % nb
    eye_parts = [jnp.where(sub_l == sub_i + SUBLANES * t, 1.0, 0.0).astype(F32) for t in range(2)]

    def bdiag64(x):
        z = jnp.zeros_like(x)
        return _cat0(jnp.where(lo64, x, z), jnp.where(lo64, z, x))

    def bdiag128(x):
        z = jnp.zeros_like(x)
        return _cat0(jnp.where(lo128, x, z), jnp.where(lo128, z, x))

    hc = lambda h: slice(h * GDN_DK, (h + 1) * GDN_DK)
    col = lambda x, h: x[:, h:h + 1]
    seqs = range(ns)
    pairs = range(GDN_PAIRS)

    def chunk(c, carry):
        rows = pl.ds(pl.multiple_of(c * CHUNK, CHUNK), CHUNK)

        gb, gcum, gcum_t = [], [], []
        for n in seqs:
            gb.append(gb_ref[n, rows, :])
            hi, mid, lo = _split3(gb[n])
            gcum.append(_dot(tril, hi) + _dot(tril, mid) + _dot(tril, lo))
            gcum_t.append(_cat0(gcum[n], gcum[n]).T)

        a_mats, attns, kdts, kbs, egs = {}, {}, {}, {}, {}
        for n in seqs:
            for p in pairs:
                ha, hb = 2 * p, 2 * p + 1
                gc_col = jnp.where(lo64, col(gcum[n], ha), col(gcum[n], hb))
                gc_row = jnp.where(lo64_row, gcum_t[n][ha:ha + 1, :], gcum_t[n][hb:hb + 1, :])
                g_last = jnp.where(lo64_row, gcum[n][CHUNK - 1:CHUNK, ha:ha + 1],
                                   gcum[n][CHUNK - 1:CHUNK, hb:hb + 1])
                decay = jnp.where(causal, jnp.exp(jnp.where(causal, gc_col - gc_row, 0.0)), 0.0)
                k_a = k_ref[n, rows, hc(ha)].astype(F32)
                k_b = k_ref[n, rows, hc(hb)].astype(F32)
                kt = _cat0(k_a, k_b).T
                kb_a = k_a * col(gb[n], GDN_HEADS + ha)
                kb_b = k_b * col(gb[n], GDN_HEADS + hb)
                lhs = _cat0(_cat1(kb_a, kb_b).astype(BF16),
                            _cat1(q_ref[n, rows, hc(ha)], q_ref[n, rows, hc(hb)]))
                kq = _dot(lhs, bdiag128(kt.astype(BF16)))
                a_mats[n, p] = jnp.where(strict, kq[:CHUNK] * decay, 0.0)
                attns[n, p] = (kq[CHUNK:] * decay).astype(BF16)
                kdts[n, p] = (kt * jnp.exp(g_last - gc_row)).astype(BF16)
                kbs[n, p] = (kb_a, kb_b)
                egs[n, p] = (jnp.exp(col(gcum[n], ha)), jnp.exp(col(gcum[n], hb)))

        cb, x = [], []
        for n in seqs:
            ad_pk = _dot_split2(_cat1(*[jnp.where(bd16, a_mats[n, p], 0.0) for p in pairs]), pack_ref[...])
            cb.append(_dot_split2(ad_pk, bcast_ref[...]))
            x.append([[eye_parts[0], eye_parts[1]] for _ in range(CHUNK // nb)])
        for j in range(nb - 1):
            tj, rj = divmod(j, SUBLANES)
            for n in seqs:
                cbj = cb[n][:, j * LANES:(j + 1) * LANES]
                for b in range(CHUNK // nb):
                    r = jnp.broadcast_to(x[n][b][tj][rj:rj + 1, :], (SUBLANES, LANES))
                    if j < SUBLANES - 1:
                        x[n][b][0] = x[n][b][0] - cbj[nb * b:nb * b + SUBLANES] * r
                    x[n][b][1] = x[n][b][1] - cbj[nb * b + SUBLANES:nb * (b + 1)] * r
        x_un = [_dot_split2(_cat0(*[part for blk in x[n] for part in blk]), unpack_ref[...]) for n in seqs]

        keys = [(n, p) for n in seqs for p in pairs]
        xd = {k: jnp.where(bd16, x_un[k[0]][:, k[1] * LANES:(k[1] + 1) * LANES], 0.0) for k in keys}
        m1 = {k: _dot(xd[k].astype(BF16), bdiag64(jnp.where(off32, a_mats[k], 0.0).astype(BF16))) for k in keys}
        x32 = {k: xd[k] - _dot(m1[k].astype(BF16), bdiag64(xd[k].astype(BF16))) for k in keys}
        m2 = {k: _dot(x32[k].astype(BF16), bdiag64(jnp.where(off64, a_mats[k], 0.0).astype(BF16))) for k in keys}
        tinv = {k: (x32[k] - _dot(m2[k].astype(BF16), bdiag64(x32[k].astype(BF16)))).astype(BF16) for k in keys}

        yws = {}
        for n, p in keys:
            ha, hb = 2 * p, 2 * p + 1
            rhs_a = _cat1(v_ref[n, rows, hc(ha)].astype(F32) * col(gb[n], GDN_HEADS + ha),
                          kbs[n, p][0] * egs[n, p][0]).astype(BF16)
            rhs_b = _cat1(v_ref[n, rows, hc(hb)].astype(F32) * col(gb[n], GDN_HEADS + hb),
                          kbs[n, p][1] * egs[n, p][1]).astype(BF16)
            z = jnp.zeros_like(rhs_a)
            yws[n, p] = _dot(tinv[n, p], _cat0(_cat1(rhs_a, z), _cat1(z, rhs_b)))

        wq = {}
        for n, p in keys:
            ha, hb = 2 * p, 2 * p + 1
            yw = yws[n, p]
            sb = state_ref[n, p].astype(BF16)
            z = jnp.zeros((GDN_DK, GDN_DK), BF16)
            s_bd = _cat0(_cat1(sb[:GDN_DK], z), _cat1(z, sb[GDN_DK:]))
            qd_a = q_ref[n, rows, hc(ha)].astype(F32) * egs[n, p][0]
            qd_b = q_ref[n, rows, hc(hb)].astype(F32) * egs[n, p][1]
            lhs = _cat0(_cat1(yw[:, 1 * GDN_DK:2 * GDN_DK], yw[:, 3 * GDN_DK:4 * GDN_DK]),
                        _cat1(qd_a, qd_b)).astype(BF16)
            wq[n, p] = _dot(lhs, s_bd)
        outs, upds = {}, {}
        for n, p in keys:
            yw = yws[n, p]
            v_new = _cat1(yw[:, 0:GDN_DK], yw[:, 2 * GDN_DK:3 * GDN_DK]) - wq[n, p][:CHUNK]
            vnb = v_new.astype(BF16)
            zc = jnp.zeros((CHUNK, GDN_DK), BF16)
            vn_bd = _cat0(_cat1(vnb[:, :GDN_DK], zc), _cat1(zc, vnb[:, GDN_DK:]))
            outs[n, p] = wq[n, p][CHUNK:] + _dot(attns[n, p], vn_bd)
            upds[n, p] = _dot(bdiag128(kdts[n, p]), _cat0(vnb[:, :GDN_DK], vnb[:, GDN_DK:]))
        for n, p in keys:
            ha, hb = 2 * p, 2 * p + 1
            e_a = jnp.exp(gcum[n][CHUNK - 1:CHUNK, ha:ha + 1])
            e_b = jnp.exp(gcum[n][CHUNK - 1:CHUNK, hb:hb + 1])
            state_ref[n, p, 0:GDN_DK, :] = state_ref[n, p, 0:GDN_DK, :] * e_a + upds[n, p][:GDN_DK]
            state_ref[n, p, GDN_DK:2 * GDN_DK, :] = (state_ref[n, p, GDN_DK:2 * GDN_DK, :] * e_b
                                                     + upds[n, p][GDN_DK:])
            for s_idx, h in enumerate((ha, hb)):
                o_h = outs[n, p][:, s_idx * GDN_DK:(s_idx + 1) * GDN_DK]
                on = _rms(o_h, on_ref[...]) * _silu(gate_ref[n, rows, hc(h)])
                o_ref[n, rows, hc(h)] = on.astype(BF16)
        return carry

    lax.fori_loop(0, tc // CHUNK, chunk, 0)


def _delta(q, k, v, gate, gb, out_norm, batch, seq, tc=256):
    m, d = q.shape
    ns = max(n for n in (4, 2, 1) if batch % n == 0)
    sel = _delta_selectors()
    row = lambda n: pl.BlockSpec((ns, tc, n), lambda b, t: (b, t, 0))
    as3d = lambda a: a.reshape(batch, seq, a.shape[-1])
    out = pl.pallas_call(
        functools.partial(_delta_kernel, tc=tc, ns=ns),
        grid=(batch // ns, seq // tc),
        in_specs=[row(d), row(d), row(d), row(d), row(LANES), _resident((1, GDN_DK))]
        + [_resident(s.shape) for s in sel],
        out_specs=row(d),
        out_shape=jax.ShapeDtypeStruct((batch, seq, d), BF16),
        scratch_shapes=[pltpu.VMEM((ns, GDN_PAIRS, 2 * GDN_DK, GDN_DK), F32)],
        compiler_params=_params(("parallel", "arbitrary")),
        name="gdn_delta",
    )(as3d(q), as3d(k), as3d(v), as3d(gate), as3d(gb), out_norm.reshape(1, GDN_DK), *sel)
    return out.reshape(m, d)


def _rope_angles(pos_ref, inv_ref):
    ang = pos_ref[...].astype(F32) * inv_ref[...]
    return jnp.cos(ang), jnp.sin(ang)


def _mla_kv_kernel(x_ref, g_ref, wc_ref, wr_ref, wrr_ref, g2_ref, wkb_ref, wvbt_ref,
                   pos_ref, inv_ref, kn_ref, kr_ref, vt_ref):
    xn = _rms(x_ref[...], g_ref[...]).astype(BF16)
    cn = _rms(_dot(xn, wc_ref[...]), g2_ref[...]).astype(BF16)
    cos, sin = _rope_angles(pos_ref, inv_ref)
    kr_ref[...] = (_dot(xn, wr_ref[...]) * cos + _dot(xn, wrr_ref[...]) * sin).astype(BF16)
    kn_ref[...] = _dot(cn, wkb_ref[...]).astype(BF16)
    vt_ref[...] = _dot_nt(wvbt_ref[...], cn).astype(BF16)


def _mla_kv(h, g, wc, wr, wrr, g2, wkb, wvbt, pos, inv, batch, seq, tm=512):
    m, d = h.shape
    n = wkb.shape[1]
    tps = seq // tm
    row = lambda w: pl.BlockSpec((tm, w), lambda i: (i, 0))
    return pl.pallas_call(
        _mla_kv_kernel,
        grid=(m // tm,),
        in_specs=[row(d), _resident((1, d)), _resident(wc.shape), _resident(wr.shape),
                  _resident(wrr.shape), _resident((1, wc.shape[1])), _resident(wkb.shape),
                  _resident(wvbt.shape), row(1), _resident((1, LANES))],
        out_specs=[row(n), row(LANES), pl.BlockSpec((None, n, tm), lambda i: (i // tps, 0, i % tps))],
        out_shape=[jax.ShapeDtypeStruct((m, n), BF16), jax.ShapeDtypeStruct((m, LANES), BF16),
                   jax.ShapeDtypeStruct((batch, n, seq), BF16)],
        compiler_params=_params(("parallel",)),
        name="mla_kv",
    )(h, g.reshape(1, d), wc, wr, wrr, g2.reshape(1, -1), wkb, wvbt, pos, inv)


def _mla_q_kernel(x_ref, g_ref, wdq_ref, g2_ref, wqn_ref, wqr_ref, wqrr_ref, pos_ref, inv_ref,
                  qn_ref, qr_ref, *, scale):
    xn = _rms(x_ref[...], g_ref[...]).astype(BF16)
    ql = _rms(_dot(xn, wdq_ref[...]), g2_ref[...]).astype(BF16)
    qn_ref[...] = (_dot(ql, wqn_ref[...]) * scale).astype(BF16)
    cos, sin = _rope_angles(pos_ref, inv_ref)
    cos = cos * scale
    sin = sin * scale
    qr = _dot(ql, wqr_ref[...])
    qrr = _dot(ql, wqrr_ref[...])
    for h in range(MLA_HEADS):
        hc = slice(h * LANES, (h + 1) * LANES)
        qr_ref[:, hc] = (qr[:, hc] * cos + qrr[:, hc] * sin).astype(BF16)


def _mla_q(h, g, wdq, g2, wqn, wqr, wqrr, pos, inv, scale, tm=512):
    m, d = h.shape
    n = wqn.shape[1]
    row = lambda w: pl.BlockSpec((tm, w), lambda i: (i, 0))
    return pl.pallas_call(
        functools.partial(_mla_q_kernel, scale=scale),
        grid=(m // tm,),
        in_specs=[row(d), _resident((1, d)), _resident(wdq.shape), _resident((1, wdq.shape[1])),
                  _resident(wqn.shape), _resident(wqr.shape), _resident(wqrr.shape), row(1),
                  _resident((1, LANES))],
        out_specs=[row(n), row(n)],
        out_shape=[jax.ShapeDtypeStruct((m, n), BF16)] * 2,
        compiler_params=_params(("parallel",)),
        name="mla_q",
    )(h, g.reshape(1, d), wdq, g2.reshape(1, -1), wqn, wqr, wqrr, pos, inv)


def _attn_kernel(qn_ref, qr_ref, kn_ref, kr_ref, vt_ref, o_ref, m_ref, l_ref, acc_ref, *, t):
    qi = pl.program_id(1)
    key_chunk = lax.broadcasted_iota(jnp.int32, (t, t), 0) // CHUNK
    qry_chunk = lax.broadcasted_iota(jnp.int32, (t, t), 1) // CHUNK
    diag_mask = qry_chunk >= key_chunk

    m_ref[...] = jnp.full(m_ref.shape, -jnp.inf, F32)
    l_ref[...] = jnp.zeros(l_ref.shape, F32)
    acc_ref[...] = jnp.zeros(acc_ref.shape, F32)

    def step(j, masked):
        rows = pl.ds(pl.multiple_of(j * t, t), t)
        kr = kr_ref[rows, :]

        def scores(h):
            hc = slice(h * LANES, (h + 1) * LANES)
            q = _cat1(qn_ref[:, hc], qr_ref[:, hc])
            return _dot_nt(_cat1(kn_ref[rows, hc], kr), q)

        s_next = scores(0)
        for h in range(MLA_HEADS):
            hc = slice(h * LANES, (h + 1) * LANES)
            s = s_next
            if h + 1 < MLA_HEADS:
                s_next = scores(h + 1)
            if masked:
                s = jnp.where(diag_mask, s, -jnp.inf)
            m_old = m_ref[h]
            m_new = jnp.maximum(m_old, jnp.max(s, axis=0, keepdims=True))
            alpha = jnp.exp(m_old - m_new)
            p = jnp.exp(s - m_new)
            l_ref[h] = alpha * l_ref[h] + jnp.sum(p, axis=0, keepdims=True)
            acc_ref[h] = alpha * acc_ref[h] + _dot(vt_ref[hc, rows], p.astype(BF16))
            m_ref[h] = m_new

    def body(j, carry):
        step(j, False)
        return carry

    lax.fori_loop(0, qi, body, 0)
    step(qi, True)
    for h in range(MLA_HEADS):
        hc = slice(h * LANES, (h + 1) * LANES)
        o_ref[:, hc] = (acc_ref[h] / l_ref[h]).T.astype(BF16)


def _attn(qn, qr, kn, kr, vt, batch, seq, t=256):
    m, d = qn.shape
    nq = seq // t
    qspec = pl.BlockSpec((t, d), lambda b, i: (b * nq + i, 0))
    kspec = lambda w: pl.BlockSpec((seq, w), lambda b, i: (b, 0))
    return pl.pallas_call(
        functools.partial(_attn_kernel, t=t),
        grid=(batch, nq),
        in_specs=[qspec, qspec, kspec(d), kspec(LANES), pl.BlockSpec((None, d, seq), lambda b, i: (b, 0, 0))],
        out_specs=qspec,
        out_shape=jax.ShapeDtypeStruct((m, d), BF16),
        scratch_shapes=[pltpu.VMEM((MLA_HEADS, 1, t), F32), pltpu.VMEM((MLA_HEADS, 1, t), F32),
                        pltpu.VMEM((MLA_HEADS, MLA_V, t), F32)],
        compiler_params=_params(("parallel", "arbitrary")),
        name="mla_attn",
    )(qn, qr, kn, kr, vt)


def _rot_half_cols(w, heads):
    k = w.shape[0]
    w = w.reshape(k, heads, 2, MLA_ROPE // 2)
    return jnp.stack([-w[:, :, 1], w[:, :, 0]], axis=2).reshape(k, heads * MLA_ROPE)


def _pad_heads(w, heads):
    k = w.shape[0]
    w = w.reshape(k, heads, MLA_ROPE)
    return jnp.pad(w, ((0, 0), (0, 0), (0, LANES - MLA_ROPE))).reshape(k, heads * LANES)


def kernel(x, positions, ffn1_norm, ffn1_w_gu, ffn1_w_down, mix_norm, ffn2_norm, ffn2_w_gu, ffn2_w_down, gdn_w_in, gdn_conv_w, gdn_a_log, gdn_dt_bias, gdn_out_norm, gdn_w_out, kv_norm, mla_w_kv_a, mla_kv_a_norm, mla_w_kv_b, mla_w_dq, mla_q_norm, mla_w_uq, mla_w_o, final_norm):
    batch, seq, d = x.shape
    m = batch * seq
    depth = ffn1_norm.shape[0]
    n_a = gdn_w_in.shape[0]
    d_qk = GDN_HEADS * GDN_DK
    bf = lambda w: w.astype(BF16)

    h = x.reshape(m, d)
    pos = positions.reshape(m, 1)
    half = MLA_ROPE // 2
    inv = ROPE_THETA ** (-jnp.arange(half, dtype=F32) / half)
    inv = jnp.pad(jnp.concatenate([inv, inv]), (0, LANES - MLA_ROPE)).reshape(1, LANES)
    scale = (MLA_NOPE + MLA_ROPE) ** -0.5

    kn = kr = vv = None
    for layer in range(depth):
        h = _ffn(h, ffn1_norm[layer], bf(ffn1_w_gu[layer]), bf(ffn1_w_down[layer]))
        if layer < n_a:
            i = layer
            w_in = gdn_w_in[i]
            wab = jnp.pad(w_in[:, 4 * d_qk:], ((0, 0), (0, LANES - 2 * GDN_HEADS)))
            pad_row = lambda p: jnp.pad(p.astype(F32), (0, LANES - GDN_HEADS)).reshape(1, LANES)
            q, k, v, gate, gb = _gdn_in(
                h, mix_norm[layer], bf(w_in[:, :3 * d_qk]), bf(w_in[:, 3 * d_qk:4 * d_qk]), bf(wab),
                gdn_conv_w[i], pad_row(gdn_a_log[i]), pad_row(gdn_dt_bias[i]), seq)
            mixed = _delta(q, k, v, gate, gb, gdn_out_norm[i], batch, seq)
            w_proj = bf(gdn_w_out[i])
        else:
            j = layer - n_a
            w_uq = mla_w_uq[j].reshape(-1, MLA_HEADS, MLA_NOPE + MLA_ROPE)
            wqn = w_uq[:, :, :MLA_NOPE].reshape(-1, MLA_HEADS * MLA_NOPE)
            wqr = w_uq[:, :, MLA_NOPE:].reshape(-1, MLA_HEADS * MLA_ROPE)
            qn, qr = _mla_q(h, mix_norm[layer], bf(mla_w_dq[j]), mla_q_norm[j], bf(wqn),
                            bf(_pad_heads(wqr, MLA_HEADS)),
                            bf(_pad_heads(_rot_half_cols(wqr, MLA_HEADS), MLA_HEADS)),
                            pos, inv, scale)
            mixed = _attn(qn, qr, kn, kr, vv, batch, seq)
            w_proj = bf(mla_w_o[j])
        last = layer == depth - 1
        h = _ffn(h, ffn2_norm[layer], bf(ffn2_w_gu[layer]), bf(ffn2_w_down[layer]),
                 proj=(mixed, w_proj), final_g=final_norm if last else None)
        if layer == n_a - 1:
            wr = mla_w_kv_a[:, MLA_KV_RANK:]
            w_kv_b = mla_w_kv_b.reshape(MLA_KV_RANK, MLA_HEADS, MLA_NOPE + MLA_V)
            kn, kr, vv = _mla_kv(
                h, kv_norm, bf(mla_w_kv_a[:, :MLA_KV_RANK]), bf(_pad_heads(wr, 1)),
                bf(_pad_heads(_rot_half_cols(wr, 1), 1)), mla_kv_a_norm,
                bf(w_kv_b[:, :, :MLA_NOPE].reshape(MLA_KV_RANK, -1)),
                bf(w_kv_b[:, :, MLA_NOPE:].reshape(MLA_KV_RANK, -1).T), pos, inv, batch, seq)
    return h.reshape(batch, seq, d)
```

```python
import functools

import jax
import jax.numpy as jnp
import numpy as np
from jax import lax
from jax.experimental import pallas as pl
from jax.experimental.pallas import tpu as pltpu

F32 = jnp.float32
BF16 = jnp.bfloat16

NORM_EPS = 1e-6
CHUNK = 64
GDN_HEADS = 8
GDN_PAIRS = GDN_HEADS // 2
GDN_DK = 128
GDN_CONV = 4
MLA_HEADS = 8
MLA_NOPE = 128
MLA_ROPE = 64
MLA_V = 128
MLA_KV_RANK = 256
MLA_QK = 256
LOG2_E = 1.4426950408889634
ROPE_THETA = 10000.0
LANES = 128
SUBLANES = 8
INV_BLOCK = 16
SCORE_LOOKAHEAD = 3
Q_TILE = 512
CONV_ROWS = 128
PROJ_COLS = 256
VMEM_LIMIT = 56 * 1024 * 1024


def _resident(shape):
    nd = len(shape)
    return pl.BlockSpec(shape, lambda *_: (0,) * nd, pipeline_mode=pl.Buffered(1))


def _params(sem):
    return pltpu.CompilerParams(dimension_semantics=sem, vmem_limit_bytes=VMEM_LIMIT)


def _rms(x, g):
    ms = jnp.mean(x * x, axis=-1, keepdims=True)
    return x * lax.rsqrt(ms + NORM_EPS) * g


def _silu(x):
    return x * jax.nn.sigmoid(x)


def _dot(a, b):
    return jnp.dot(a, b, preferred_element_type=F32)


def _dot_nt(a, b):
    return lax.dot_general(a, b, (((1,), (1,)), ((), ())), preferred_element_type=F32)


def _cat0(*xs):
    return jnp.concatenate(xs, axis=0)


def _cat1(*xs):
    return jnp.concatenate(xs, axis=1)


def _ffn_kernel(*refs, d_ff, tf, proj, final):
    refs = list(refs)
    x_ref = refs.pop(0)
    if proj:
        a_ref, wp_ref = refs.pop(0), refs.pop(0)
    g_ref, wgu_ref, wd_ref = refs.pop(0), refs.pop(0), refs.pop(0)
    if final:
        gf_ref = refs.pop(0)
    o_ref, xn_ref, act_ref = refs
    if proj:
        o_ref[...] = x_ref[...] + _dot(a_ref[...], wp_ref[...])
        h_ref = o_ref
    else:
        h_ref = x_ref
    xn_ref[...] = _rms(h_ref[...], g_ref[...]).astype(BF16)
    for c in range(d_ff // tf):
        g = _dot(xn_ref[...], wgu_ref[:, c * tf:(c + 1) * tf])
        u = _dot(xn_ref[...], wgu_ref[:, d_ff + c * tf:d_ff + (c + 1) * tf])
        act_ref[:, c * tf:(c + 1) * tf] = (_silu(g) * u).astype(BF16)
    y = h_ref[...] + 0.5 * _dot(act_ref[...], wd_ref[...])
    if final:
        y = _rms(y, gf_ref[...])
    o_ref[...] = y


def _ffn(h, g, wgu, wd, proj=None, final_g=None, tm=512, tf=256):
    m, d = h.shape
    d_ff = wd.shape[0]
    row = lambda n: pl.BlockSpec((tm, n), lambda i: (i, 0))
    in_specs, args = [row(d)], [h]
    if proj is not None:
        a, wp = proj
        in_specs += [row(a.shape[1]), _resident(wp.shape)]
        args += [a, wp]
    in_specs += [_resident((1, d)), _resident(wgu.shape), _resident(wd.shape)]
    args += [g.reshape(1, d), wgu, wd]
    if final_g is not None:
        in_specs.append(_resident((1, d)))
        args.append(final_g.reshape(1, d))
    return pl.pallas_call(
        functools.partial(_ffn_kernel, d_ff=d_ff, tf=tf, proj=proj is not None,
                          final=final_g is not None),
        grid=(m // tm,),
        in_specs=in_specs,
        out_specs=row(d),
        out_shape=jax.ShapeDtypeStruct((m, d), F32),
        scratch_shapes=[pltpu.VMEM((tm, d), BF16), pltpu.VMEM((tm, d_ff), BF16)],
        compiler_params=_params(("parallel",)),
        name="ffn" + ("_proj" if proj is not None else "") + ("_final" if final_g is not None else ""),
    )(*args)


def _gdn_in_kernel(x_ref, g_ref, wqkv_ref, wgate_ref, wab_ref, cw_ref, alog_ref, dtb_ref,
                   q_ref, k_ref, v_ref, gate_ref, gb_ref, xn_ref, pbuf_ref,
                   *, tm, tiles_per_seq, d_qk):
    i = pl.program_id(0)
    xn_ref[...] = _rms(x_ref[...], g_ref[...]).astype(BF16)

    @pl.when(i % tiles_per_seq == 0)
    def _():
        pbuf_ref[0:SUBLANES, :] = jnp.zeros((SUBLANES, pbuf_ref.shape[1]), F32)

    outs = (q_ref, k_ref, v_ref)
    heads_per_piece = PROJ_COLS // GDN_DK
    pieces = [(part, g) for part in range(3) for g in range(d_qk // PROJ_COLS)]

    def project(part, g):
        cols = slice(part * d_qk + g * PROJ_COLS, part * d_qk + (g + 1) * PROJ_COLS)
        pbuf_ref[SUBLANES:SUBLANES + tm, cols] = _dot(xn_ref[...], wqkv_ref[:, cols])

    def conv_act_norm(part, g):
        scale = GDN_DK ** -0.5 if part == 0 else 1.0
        for h in range(g * heads_per_piece, (g + 1) * heads_per_piece):
            hc = slice(h * GDN_DK, (h + 1) * GDN_DK)
            cols = slice(part * d_qk + h * GDN_DK, part * d_qk + (h + 1) * GDN_DK)
            for r0 in range(0, tm, CONV_ROWS):
                y = cw_ref[GDN_CONV - 1:GDN_CONV, cols] * pbuf_ref[SUBLANES + r0:SUBLANES + r0 + CONV_ROWS, cols]
                for j in range(GDN_CONV - 1):
                    off = SUBLANES - (GDN_CONV - 1) + j + r0
                    y = y + cw_ref[j:j + 1, cols] * pbuf_ref[off:off + CONV_ROWS, cols]
                y = _silu(y)
                if part < 2:
                    ss = jnp.sum(y * y, axis=-1, keepdims=True)
                    y = y * (lax.rsqrt(ss + NORM_EPS) * scale)
                outs[part][r0:r0 + CONV_ROWS, hc] = y.astype(BF16)

    def gate(g):
        cols = slice(g * PROJ_COLS, (g + 1) * PROJ_COLS)
        gate_ref[:, cols] = _dot(xn_ref[...], wgate_ref[:, cols]).astype(gate_ref.dtype)

    n_gate = d_qk // PROJ_COLS
    project(*pieces[0])
    for idx, piece in enumerate(pieces):
        if idx + 1 < len(pieces):
            project(*pieces[idx + 1])
        elif n_gate:
            gate(0)
        conv_act_norm(*piece)
        tail = idx - (len(pieces) - n_gate)
        if 0 <= tail < n_gate - 1:
            gate(tail + 1)
    pbuf_ref[0:SUBLANES, :] = pbuf_ref[tm:tm + SUBLANES, :]

    ab = _dot(xn_ref[...], wab_ref[...])
    x = ab + dtb_ref[...]
    softplus = jnp.maximum(x, 0.0) + jnp.log1p(jnp.exp(-jnp.abs(x)))
    decay = -jnp.exp(alog_ref[...]) * softplus
    lane = lax.broadcasted_iota(jnp.int32, ab.shape, 1)
    gb_ref[...] = jnp.where(lane < GDN_HEADS, decay, jax.nn.sigmoid(ab))


def _gdn_in(h, g, wqkv, wgate, wab, conv_w, alog_row, dtb_row, seq, tm=512):
    m, d = h.shape
    d_qk = wgate.shape[1]
    row = lambda n: pl.BlockSpec((tm, n), lambda i: (i, 0))
    return pl.pallas_call(
        functools.partial(_gdn_in_kernel, tm=tm, tiles_per_seq=seq // tm, d_qk=d_qk),
        grid=(m // tm,),
        in_specs=[row(d), _resident((1, d)), _resident(wqkv.shape), _resident(wgate.shape),
                  _resident(wab.shape), _resident(conv_w.shape), _resident((1, LANES)),
                  _resident((1, LANES))],
        out_specs=[row(d_qk), row(d_qk), row(d_qk), row(d_qk), row(LANES)],
        out_shape=[jax.ShapeDtypeStruct((m, d_qk), BF16)] * 4 + [jax.ShapeDtypeStruct((m, LANES), F32)],
        scratch_shapes=[pltpu.VMEM((tm, d), BF16), pltpu.VMEM((tm + SUBLANES, 3 * d_qk), F32)],
        compiler_params=_params(("arbitrary",)),
        name="gdn_in",
    )(h, g.reshape(1, d), wqkv, wgate, wab, conv_w, alog_row, dtb_row)


def _split2(x):
    hi = x.astype(BF16)
    mid = (x - hi.astype(F32)).astype(BF16)
    return hi, mid


def _split3(x):
    hi = x.astype(BF16)
    r = x - hi.astype(F32)
    mid = r.astype(BF16)
    lo = (r - mid.astype(F32)).astype(BF16)
    return hi, mid, lo


def _dot_split2(x, w):
    hi, mid = _split2(x)
    r = _dot(_cat0(hi, mid), w)
    n = x.shape[0]
    return r[:n] + r[n:]


def _delta_selectors():
    nb = INV_BLOCK
    pack = np.zeros((GDN_PAIRS * LANES, LANES), np.float32)
    for r in range(GDN_PAIRS * LANES):
        p, l = divmod(r, LANES)
        h, c = 2 * p + l // CHUNK, l % CHUNK
        pack[r, nb * h + c % nb] = 1.0
    bcast = np.zeros((LANES, (nb - 1) * LANES), np.float32)
    for j in range(nb - 1):
        for c in range(LANES):
            bcast[nb * (c // nb) + j, LANES * j + c] = 1.0
    unpack = np.zeros((LANES, GDN_HEADS * CHUNK), np.float32)
    for h in range(GDN_HEADS):
        for c in range(CHUNK):
            unpack[nb * h + c % nb, CHUNK * h + c] = 1.0
    return tuple(jnp.asarray(a, BF16) for a in (pack, bcast, unpack))


def _delta_kernel(q_ref, k_ref, v_ref, gate_ref, gb_ref, on_ref, pack_ref, bcast_ref, unpack_ref,
                  o_ref, state_ref, *, tc, ns):
    @pl.when(pl.program_id(1) == 0)
    def _():
        state_ref[...] = jnp.zeros(state_ref.shape, F32)

    nb = INV_BLOCK
    ri = lax.broadcasted_iota(jnp.int32, (CHUNK, LANES), 0)
    li = lax.broadcasted_iota(jnp.int32, (CHUNK, LANES), 1)
    ci = li % CHUNK
    lo64 = li < CHUNK
    lo64_row = lo64[0:1]
    causal = ri >= ci
    strict = ri > ci
    bd16 = (ri // nb) == (ci // nb)
    bd32 = (ri // (2 * nb)) == (ci // (2 * nb))
    off32 = jnp.logical_and(bd32, jnp.logical_not(bd16))
    off64 = jnp.logical_not(bd32)
    lo128 = lax.broadcasted_iota(jnp.int32, (LANES, LANES), 1) < CHUNK
    tril = jnp.where(lax.broadcasted_iota(jnp.int32, (CHUNK, CHUNK), 0)
                     >= lax.broadcasted_iota(jnp.int32, (CHUNK, CHUNK), 1), 1.0, 0.0).astype(BF16)
    sub_i = lax.broadcasted_iota(jnp.int32, (SUBLANES, LANES), 0)
    sub_l = lax.broadcasted_iota(jnp.int32, (SUBLANES, LANES), 1) % nb
    eye_parts = [jnp.where(sub_l == sub_i + SUBLANES * t, 1.0, 0.0).astype(F32) for t in range(2)]

    def bdiag64(x):
        z = jnp.zeros_like(x)
        return _cat0(jnp.where(lo64, x, z), jnp.where(lo64, z, x))

    def bdiag128(x):
        z = jnp.zeros_like(x)
        return _cat0(jnp.where(lo128, x, z), jnp.where(lo128, z, x))

    hc = lambda h: slice(h * GDN_DK, (h + 1) * GDN_DK)
    col = lambda x, h: x[:, h:h + 1]
    seqs = range(ns)
    pairs = range(GDN_PAIRS)

    def chunk(c, carry):
        rows = pl.ds(pl.multiple_of(c * CHUNK, CHUNK), CHUNK)

        gb, gcum, gcum_t = [], [], []
        for n in seqs:
            gb.append(gb_ref[n, rows, :])
            hi, mid, lo = _split3(gb[n])
            gcum.append(_dot(tril, hi) + _dot(tril, mid) + _dot(tril, lo))
            gcum_t.append(_cat0(gcum[n], gcum[n]).T)

        a_mats, attns, kdts, kbs, egs = {}, {}, {}, {}, {}
        for n in seqs:
            for p in pairs:
                ha, hb = 2 * p, 2 * p + 1
                gc_col = jnp.where(lo64, col(gcum[n], ha), col(gcum[n], hb))
                gc_row = jnp.where(lo64_row, gcum_t[n][ha:ha + 1, :], gcum_t[n][hb:hb + 1, :])
                g_last = jnp.where(lo64_row, gcum[n][CHUNK - 1:CHUNK, ha:ha + 1],
                                   gcum[n][CHUNK - 1:CHUNK, hb:hb + 1])
                decay = jnp.where(causal, jnp.exp(jnp.where(causal, gc_col - gc_row, 0.0)), 0.0)
                k_a = k_ref[n, rows, hc(ha)].astype(F32)
                k_b = k_ref[n, rows, hc(hb)].astype(F32)
                kt = _cat0(k_a, k_b).T
                kb_a = k_a * col(gb[n], GDN_HEADS + ha)
                kb_b = k_b * col(gb[n], GDN_HEADS + hb)
                lhs = _cat0(_cat1(kb_a, kb_b).astype(BF16),
                            _cat1(q_ref[n, rows, hc(ha)], q_ref[n, rows, hc(hb)]))
                kq = _dot(lhs, bdiag128(kt.astype(BF16)))
                a_mats[n, p] = jnp.where(strict, kq[:CHUNK] * decay, 0.0)
                attns[n, p] = (kq[CHUNK:] * decay).astype(BF16)
                kdts[n, p] = (kt * jnp.exp(g_last - gc_row)).astype(BF16)
                kbs[n, p] = (kb_a, kb_b)
                egs[n, p] = (jnp.exp(col(gcum[n], ha)), jnp.exp(col(gcum[n], hb)))

        cb, x = [], []
        for n in seqs:
            ad_pk = _dot_split2(_cat1(*[jnp.where(bd16, a_mats[n, p], 0.0) for p in pairs]), pack_ref[...])
            cb.append(_dot_split2(ad_pk, bcast_ref[...]))
            x.append([[eye_parts[0], eye_parts[1]] for _ in range(CHUNK // nb)])
        for j in range(nb - 1):
            tj, rj = divmod(j, SUBLANES)
            for n in seqs:
                cbj = cb[n][:, j * LANES:(j + 1) * LANES]
                for b in range(CHUNK // nb):
                    r = jnp.broadcast_to(x[n][b][tj][rj:rj + 1, :], (SUBLANES, LANES))
                    if j < SUBLANES - 1:
                        x[n][b][0] = x[n][b][0] - cbj[nb * b:nb * b + SUBLANES] * r
                    x[n][b][1] = x[n][b][1] - cbj[nb * b + SUBLANES:nb * (b + 1)] * r
        x_un = [_dot_split2(_cat0(*[part for blk in x[n] for part in blk]), unpack_ref[...]) for n in seqs]

        keys = [(n, p) for n in seqs for p in pairs]
        xd = {k: jnp.where(bd16, x_un[k[0]][:, k[1] * LANES:(k[1] + 1) * LANES], 0.0) for k in keys}
        m1 = {k: _dot(xd[k].astype(BF16), bdiag64(jnp.where(off32, a_mats[k], 0.0).astype(BF16))) for k in keys}
        x32 = {k: xd[k] - _dot(m1[k].astype(BF16), bdiag64(xd[k].astype(BF16))) for k in keys}
        m2 = {k: _dot(x32[k].astype(BF16), bdiag64(jnp.where(off64, a_mats[k], 0.0).astype(BF16))) for k in keys}
        tinv = {k: (x32[k] - _dot(m2[k].astype(BF16), bdiag64(x32[k].astype(BF16)))).astype(BF16) for k in keys}

        yws = {}
        for n, p in keys:
            ha, hb = 2 * p, 2 * p + 1
            rhs_a = _cat1(v_ref[n, rows, hc(ha)].astype(F32) * col(gb[n], GDN_HEADS + ha),
                          kbs[n, p][0] * egs[n, p][0]).astype(BF16)
            rhs_b = _cat1(v_ref[n, rows, hc(hb)].astype(F32) * col(gb[n], GDN_HEADS + hb),
                          kbs[n, p][1] * egs[n, p][1]).astype(BF16)
            z = jnp.zeros_like(rhs_a)
            yws[n, p] = _dot(tinv[n, p], _cat0(_cat1(rhs_a, z), _cat1(z, rhs_b)))

        wq = {}
        for n, p in keys:
            ha, hb = 2 * p, 2 * p + 1
            yw = yws[n, p]
            sb = state_ref[n, p].astype(BF16)
            z = jnp.zeros((GDN_DK, GDN_DK), BF16)
            s_bd = _cat0(_cat1(sb[:GDN_DK], z), _cat1(z, sb[GDN_DK:]))
            qd_a = q_ref[n, rows, hc(ha)].astype(F32) * egs[n, p][0]
            qd_b = q_ref[n, rows, hc(hb)].astype(F32) * egs[n, p][1]
            lhs = _cat0(_cat1(yw[:, 1 * GDN_DK:2 * GDN_DK], yw[:, 3 * GDN_DK:4 * GDN_DK]),
                        _cat1(qd_a, qd_b)).astype(BF16)
            wq[n, p] = _dot(lhs, s_bd)
        outs, upds = {}, {}
        for n, p in keys:
            yw = yws[n, p]
            v_new = _cat1(yw[:, 0:GDN_DK], yw[:, 2 * GDN_DK:3 * GDN_DK]) - wq[n, p][:CHUNK]
            vnb = v_new.astype(BF16)
            zc = jnp.zeros((CHUNK, GDN_DK), BF16)
            vn_bd = _cat0(_cat1(vnb[:, :GDN_DK], zc), _cat1(zc, vnb[:, GDN_DK:]))
            outs[n, p] = wq[n, p][CHUNK:] + _dot(attns[n, p], vn_bd)
            upds[n, p] = _dot(bdiag128(kdts[n, p]), _cat0(vnb[:, :GDN_DK], vnb[:, GDN_DK:]))
        for n, p in keys:
            ha, hb = 2 * p, 2 * p + 1
            e_a = jnp.exp(gcum[n][CHUNK - 1:CHUNK, ha:ha + 1])
            e_b = jnp.exp(gcum[n][CHUNK - 1:CHUNK, hb:hb + 1])
            state_ref[n, p, 0:GDN_DK, :] = state_ref[n, p, 0:GDN_DK, :] * e_a + upds[n, p][:GDN_DK]
            state_ref[n, p, GDN_DK:2 * GDN_DK, :] = (state_ref[n, p, GDN_DK:2 * GDN_DK, :] * e_b
                                                     + upds[n, p][GDN_DK:])
            for s_idx, h in enumerate((ha, hb)):
                o_h = outs[n, p][:, s_idx * GDN_DK:(s_idx + 1) * GDN_DK]
                on = _rms(o_h, on_ref[...]) * _silu(gate_ref[n, rows, hc(h)].astype(F32))
                o_ref[n, rows, hc(h)] = on.astype(BF16)
        return carry

    lax.fori_loop(0, tc // CHUNK, chunk, 0)


def _delta(q, k, v, gate, gb, out_norm, batch, seq, tc=256):
    m, d = q.shape
    ns = max(n for n in (4, 2, 1) if batch % n == 0)
    sel = _delta_selectors()
    row = lambda n: pl.BlockSpec((ns, tc, n), lambda b, t: (b, t, 0))
    as3d = lambda a: a.reshape(batch, seq, a.shape[-1])
    out = pl.pallas_call(
        functools.partial(_delta_kernel, tc=tc, ns=ns),
        grid=(batch // ns, seq // tc),
        in_specs=[row(d), row(d), row(d), row(d), row(LANES), _resident((1, GDN_DK))]
        + [_resident(s.shape) for s in sel],
        out_specs=row(d),
        out_shape=jax.ShapeDtypeStruct((batch, seq, d), BF16),
        scratch_shapes=[pltpu.VMEM((ns, GDN_PAIRS, 2 * GDN_DK, GDN_DK), F32)],
        compiler_params=_params(("parallel", "arbitrary")),
        name="gdn_delta",
    )(as3d(q), as3d(k), as3d(v), as3d(gate), as3d(gb), out_norm.reshape(1, GDN_DK), *sel)
    return out.reshape(m, d)


def _rope_angles(pos_ref, inv_ref):
    ang = pos_ref[...].astype(F32) * inv_ref[...]
    return jnp.cos(ang), jnp.sin(ang)


def _mla_kv_kernel(x_ref, g_ref, wc_ref, wr_ref, wrr_ref, g2_ref, wkb_ref, wvbt_ref,
                   pos_ref, inv_ref, k_ref, vt_ref):
    xn = _rms(x_ref[...], g_ref[...]).astype(BF16)
    cn = _rms(_dot(xn, wc_ref[...]), g2_ref[...]).astype(BF16)
    cos, sin = _rope_angles(pos_ref, inv_ref)
    kr = _dot(xn, wr_ref[...]) * cos + _dot(xn, wrr_ref[...]) * sin
    low = lax.broadcasted_iota(jnp.int32, kr.shape, 1) < MLA_ROPE
    kr_halves = (jnp.where(low, kr, 0.0).astype(BF16), jnp.where(low, 0.0, kr).astype(BF16))
    kn = _dot(cn, wkb_ref[...]).astype(BF16)
    for h in range(MLA_HEADS):
        k_ref[:, h * MLA_QK:h * MLA_QK + MLA_NOPE] = kn[:, h * MLA_NOPE:(h + 1) * MLA_NOPE]
        k_ref[:, h * MLA_QK + MLA_NOPE:(h + 1) * MLA_QK] = kr_halves[h % 2]
    vt_ref[...] = _dot_nt(wvbt_ref[...], cn).astype(BF16)


def _mla_kv(h, g, wc, wr, wrr, g2, wkb, wvbt, pos, inv, batch, seq, tm=512):
    m, d = h.shape
    n = wkb.shape[1]
    tps = seq // tm
    row = lambda w: pl.BlockSpec((tm, w), lambda i: (i, 0))
    return pl.pallas_call(
        _mla_kv_kernel,
        grid=(m // tm,),
        in_specs=[row(d), _resident((1, d)), _resident(wc.shape), _resident(wr.shape),
                  _resident(wrr.shape), _resident((1, wc.shape[1])), _resident(wkb.shape),
                  _resident(wvbt.shape), row(1), _resident((1, LANES))],
        out_specs=[row(MLA_HEADS * MLA_QK), pl.BlockSpec((None, n, tm), lambda i: (i // tps, 0, i % tps))],
        out_shape=[jax.ShapeDtypeStruct((m, MLA_HEADS * MLA_QK), BF16),
                   jax.ShapeDtypeStruct((batch, n, seq), BF16)],
        compiler_params=_params(("parallel",)),
        name="mla_kv",
    )(h, g.reshape(1, d), wc, wr, wrr, g2.reshape(1, -1), wkb, wvbt, pos, inv)


def _mla_q_kernel(x_ref, g_ref, wdq_ref, g2_ref, wqnt_ref, wqrt_ref, wqrrt_ref, pos_ref, inv_ref,
                  qt_ref, *, scale):
    xn = _rms(x_ref[...], g_ref[...]).astype(BF16)
    ql = _rms(_dot(xn, wdq_ref[...]), g2_ref[...]).astype(BF16)
    qnt = (_dot_nt(wqnt_ref[...], ql) * scale).astype(BF16)
    ang = inv_ref[...] * pos_ref[...].astype(F32)
    cos = jnp.cos(ang) * scale
    sin = jnp.sin(ang) * scale
    qrt = _dot_nt(wqrt_ref[...], ql)
    qrrt = _dot_nt(wqrrt_ref[...], ql)
    low = lax.broadcasted_iota(jnp.int32, cos.shape, 0) < MLA_ROPE
    for p in range(MLA_HEADS // 2):
        pr = slice(p * LANES, (p + 1) * LANES)
        r = qrt[pr] * cos + qrrt[pr] * sin
        halves = (jnp.where(low, r, 0.0).astype(BF16), jnp.where(low, 0.0, r).astype(BF16))
        for s in range(2):
            h = 2 * p + s
            qt_ref[h * MLA_QK:h * MLA_QK + MLA_NOPE, :] = qnt[h * MLA_NOPE:(h + 1) * MLA_NOPE]
            qt_ref[h * MLA_QK + MLA_NOPE:(h + 1) * MLA_QK, :] = halves[s]


def _mla_q(h, g, wdq, g2, wqnt, wqrt, wqrrt, pos_rows, inv_col, scale, batch, seq, tm=Q_TILE):
    m, d = h.shape
    tps = seq // tm
    row = lambda w: pl.BlockSpec((tm, w), lambda i: (i, 0))
    return pl.pallas_call(
        functools.partial(_mla_q_kernel, scale=scale),
        grid=(m // tm,),
        in_specs=[row(d), _resident((1, d)), _resident(wdq.shape), _resident((1, wdq.shape[1])),
                  _resident(wqnt.shape), _resident(wqrt.shape), _resident(wqrrt.shape),
                  pl.BlockSpec((None, 1, tm), lambda i: (i, 0, 0)), _resident((LANES, 1))],
        out_specs=pl.BlockSpec((None, MLA_HEADS * MLA_QK, tm), lambda i: (i // tps, 0, i % tps)),
        out_shape=jax.ShapeDtypeStruct((batch, MLA_HEADS * MLA_QK, seq), BF16),
        compiler_params=_params(("parallel",)),
        name="mla_q",
    )(h, g.reshape(1, d), wdq, g2.reshape(1, -1), wqnt, wqrt, wqrrt, pos_rows, inv_col)


def _attn_kernel(qt_ref, k_ref, vt_ref, o_ref, m_ref, l_ref, acc_ref, *, t):
    qi = pl.program_id(1)
    key_chunk = lax.broadcasted_iota(jnp.int32, (t, t), 0) // CHUNK
    qry_chunk = lax.broadcasted_iota(jnp.int32, (t, t), 1) // CHUNK
    diag_mask = qry_chunk >= key_chunk

    m_ref[...] = jnp.full(m_ref.shape, -jnp.inf, F32)
    l_ref[...] = jnp.zeros(l_ref.shape, F32)
    acc_ref[...] = jnp.zeros(acc_ref.shape, F32)

    def scores(j, h):
        rows = pl.ds(pl.multiple_of(j * t, t), t)
        hq = slice(h * MLA_QK, (h + 1) * MLA_QK)
        return _dot(k_ref[rows, hq], qt_ref[hq, :])

    def step(j, pending, masked):
        rows = pl.ds(pl.multiple_of(j * t, t), t)
        pending = list(pending)
        for h in range(MLA_HEADS):
            hv = slice(h * MLA_V, (h + 1) * MLA_V)
            s = pending.pop(0)
            ahead = h + SCORE_LOOKAHEAD
            if ahead < MLA_HEADS:
                pending.append(scores(j, ahead))
            elif not masked:
                pending.append(scores(j + 1, ahead - MLA_HEADS))
            if masked:
                s = jnp.where(diag_mask, s, -jnp.inf)
            m_old = m_ref[h]
            m_new = jnp.maximum(m_old, jnp.max(s, axis=0, keepdims=True))
            alpha = jnp.exp2(m_old - m_new)
            p = jnp.exp2(s - m_new)
            l_ref[h] = alpha * l_ref[h] + jnp.sum(p, axis=0, keepdims=True)
            acc_ref[h] = alpha * acc_ref[h] + _dot(vt_ref[hv, rows], p.astype(BF16))
            m_ref[h] = m_new
        return tuple(pending)

    pending = tuple(scores(0, h) for h in range(SCORE_LOOKAHEAD))
    pending = lax.fori_loop(0, qi, lambda j, c: step(j, c, False), pending)
    step(qi, pending, True)
    for h in range(MLA_HEADS):
        o_ref[:, h * MLA_V:(h + 1) * MLA_V] = (acc_ref[h] / l_ref[h]).T.astype(BF16)


def _attn(qt, k, vt, batch, seq, t=256):
    m, dq = k.shape
    dv = vt.shape[1]
    nq = seq // t
    return pl.pallas_call(
        functools.partial(_attn_kernel, t=t),
        grid=(batch, nq),
        in_specs=[pl.BlockSpec((None, dq, t), lambda b, i: (b, 0, i)),
                  pl.BlockSpec((seq, dq), lambda b, i: (b, 0)),
                  pl.BlockSpec((None, dv, seq), lambda b, i: (b, 0, 0))],
        out_specs=pl.BlockSpec((t, dv), lambda b, i: (b * nq + i, 0)),
        out_shape=jax.ShapeDtypeStruct((m, dv), BF16),
        scratch_shapes=[pltpu.VMEM((MLA_HEADS, 1, t), F32), pltpu.VMEM((MLA_HEADS, 1, t), F32),
                        pltpu.VMEM((MLA_HEADS, MLA_V, t), F32)],
        compiler_params=_params(("parallel", "arbitrary")),
        name="mla_attn",
    )(qt, k, vt)


def _rot_half_cols(w, heads):
    k = w.shape[0]
    w = w.reshape(k, heads, 2, MLA_ROPE // 2)
    return jnp.stack([-w[:, :, 1], w[:, :, 0]], axis=2).reshape(k, heads * MLA_ROPE)


def kernel(x, positions, ffn1_norm, ffn1_w_gu, ffn1_w_down, mix_norm, ffn2_norm, ffn2_w_gu, ffn2_w_down, gdn_w_in, gdn_conv_w, gdn_a_log, gdn_dt_bias, gdn_out_norm, gdn_w_out, kv_norm, mla_w_kv_a, mla_kv_a_norm, mla_w_kv_b, mla_w_dq, mla_q_norm, mla_w_uq, mla_w_o, final_norm):
    batch, seq, d = x.shape
    m = batch * seq
    depth = ffn1_norm.shape[0]
    n_a = gdn_w_in.shape[0]
    d_qk = GDN_HEADS * GDN_DK
    bf = lambda w: w.astype(BF16)

    h = x.reshape(m, d)
    pos = positions.reshape(m, 1)
    pos_rows = positions.reshape(m // Q_TILE, 1, Q_TILE)
    half = MLA_ROPE // 2
    inv = ROPE_THETA ** (-jnp.arange(half, dtype=F32) / half)
    inv = jnp.tile(inv, LANES // half).reshape(1, LANES)
    scale = (MLA_NOPE + MLA_ROPE) ** -0.5 * LOG2_E

    kk = vt = None
    for layer in range(depth):
        h = _ffn(h, ffn1_norm[layer], bf(ffn1_w_gu[layer]), bf(ffn1_w_down[layer]))
        if layer < n_a:
            i = layer
            w_in = gdn_w_in[i]
            wab = jnp.pad(w_in[:, 4 * d_qk:], ((0, 0), (0, LANES - 2 * GDN_HEADS)))
            pad_row = lambda p: jnp.pad(p.astype(F32), (0, LANES - GDN_HEADS)).reshape(1, LANES)
            q, k, v, gate, gb = _gdn_in(
                h, mix_norm[layer], bf(w_in[:, :3 * d_qk]), bf(w_in[:, 3 * d_qk:4 * d_qk]), bf(wab),
                gdn_conv_w[i], pad_row(gdn_a_log[i]), pad_row(gdn_dt_bias[i]), seq)
            mixed = _delta(q, k, v, gate, gb, gdn_out_norm[i], batch, seq)
            w_proj = bf(gdn_w_out[i])
        else:
            j = layer - n_a
            w_uq = mla_w_uq[j].reshape(-1, MLA_HEADS, MLA_NOPE + MLA_ROPE)
            wqn = w_uq[:, :, :MLA_NOPE].reshape(-1, MLA_HEADS * MLA_NOPE)
            wqr = w_uq[:, :, MLA_NOPE:].reshape(-1, MLA_HEADS * MLA_ROPE)
            qt = _mla_q(h, mix_norm[layer], bf(mla_w_dq[j]), mla_q_norm[j], bf(wqn.T), bf(wqr.T),
                        bf(_rot_half_cols(wqr, MLA_HEADS).T), pos_rows, inv.reshape(LANES, 1), scale,
                        batch, seq)
            mixed = _attn(qt, kk, vt, batch, seq)
            w_proj = bf(mla_w_o[j])
        last = layer == depth - 1
        h = _ffn(h, ffn2_norm[layer], bf(ffn2_w_gu[layer]), bf(ffn2_w_down[layer]),
                 proj=(mixed, w_proj), final_g=final_norm if last else None)
        if layer == n_a - 1:
            wr = mla_w_kv_a[:, MLA_KV_RANK:]
            w_kv_b = mla_w_kv_b.reshape(MLA_KV_RANK, MLA_HEADS, MLA_NOPE + MLA_V)
            twice = lambda w: jnp.concatenate([w, w], axis=1)
            kk, vt = _mla_kv(
                h, kv_norm, bf(mla_w_kv_a[:, :MLA_KV_RANK]), bf(twice(wr)),
                bf(twice(_rot_half_cols(wr, 1))), mla_kv_a_norm,
                bf(w_kv_b[:, :, :MLA_NOPE].reshape(MLA_KV_RANK, -1)),
                bf(w_kv_b[:, :, MLA_NOPE:].reshape(MLA_KV_RANK, -1).T), pos, inv, batch, seq)
    return h.reshape(batch, seq, d)
```

```python
import functools

import jax
import jax.numpy as jnp
import numpy as np
from jax import lax
from jax.experimental import pallas as pl
from jax.experimental.pallas import tpu as pltpu

F32 = jnp.float32
BF16 = jnp.bfloat16

NORM_EPS = 1e-6
CHUNK = 64
GDN_HEADS = 8
GDN_PAIRS = GDN_HEADS // 2
GDN_DK = 128
GDN_CONV = 4
MLA_HEADS = 8
MLA_NOPE = 128
MLA_ROPE = 64
MLA_V = 128
MLA_VX = MLA_V + 16
MLA_KV_RANK = 256
MLA_QK = 256
LOG2_E = 1.4426950408889634
ROPE_THETA = 10000.0
LANES = 128
SUBLANES = 8
INV_BLOCK = 16
SCORE_LOOKAHEAD = 3
Q_TILE = 512
CONV_ROWS = 128
PROJ_COLS = 256
VMEM_LIMIT = 56 * 1024 * 1024


def _resident(shape):
    nd = len(shape)
    return pl.BlockSpec(shape, lambda *_: (0,) * nd, pipeline_mode=pl.Buffered(1))


def _params(sem):
    return pltpu.CompilerParams(dimension_semantics=sem, vmem_limit_bytes=VMEM_LIMIT)


def _rms(x, g):
    ms = jnp.mean(x * x, axis=-1, keepdims=True)
    return x * lax.rsqrt(ms + NORM_EPS) * g


def _silu(x):
    return x * jax.nn.sigmoid(x)


def _dot(a, b):
    return jnp.dot(a, b, preferred_element_type=F32)


def _dot_nt(a, b):
    return lax.dot_general(a, b, (((1,), (1,)), ((), ())), preferred_element_type=F32)


def _cat0(*xs):
    return jnp.concatenate(xs, axis=0)


def _cat1(*xs):
    return jnp.concatenate(xs, axis=1)


def _ffn_kernel(*refs, d_ff, tf, proj, final):
    refs = list(refs)
    x_ref = refs.pop(0)
    if proj:
        a_ref, wp_ref = refs.pop(0), refs.pop(0)
    g_ref, wgu_ref, wd_ref = refs.pop(0), refs.pop(0), refs.pop(0)
    if final:
        gf_ref = refs.pop(0)
    o_ref, xn_ref, act_ref = refs
    if proj:
        o_ref[...] = x_ref[...] + _dot(a_ref[...], wp_ref[...])
        h_ref = o_ref
    else:
        h_ref = x_ref
    xn_ref[...] = _rms(h_ref[...], g_ref[...]).astype(BF16)
    for c in range(d_ff // tf):
        g = _dot(xn_ref[...], wgu_ref[:, c * tf:(c + 1) * tf])
        u = _dot(xn_ref[...], wgu_ref[:, d_ff + c * tf:d_ff + (c + 1) * tf])
        act_ref[:, c * tf:(c + 1) * tf] = (_silu(g) * u).astype(BF16)
    y = h_ref[...] + 0.5 * _dot(act_ref[...], wd_ref[...])
    if final:
        y = _rms(y, gf_ref[...])
    o_ref[...] = y


def _ffn(h, g, wgu, wd, proj=None, final_g=None, tm=512, tf=256):
    m, d = h.shape
    d_ff = wd.shape[0]
    row = lambda n: pl.BlockSpec((tm, n), lambda i: (i, 0))
    in_specs, args = [row(d)], [h]
    if proj is not None:
        a, wp = proj
        in_specs += [row(a.shape[1]), _resident(wp.shape)]
        args += [a, wp]
    in_specs += [_resident((1, d)), _resident(wgu.shape), _resident(wd.shape)]
    args += [g.reshape(1, d), wgu, wd]
    if final_g is not None:
        in_specs.append(_resident((1, d)))
        args.append(final_g.reshape(1, d))
    return pl.pallas_call(
        functools.partial(_ffn_kernel, d_ff=d_ff, tf=tf, proj=proj is not None,
                          final=final_g is not None),
        grid=(m // tm,),
        in_specs=in_specs,
        out_specs=row(d),
        out_shape=jax.ShapeDtypeStruct((m, d), F32),
        scratch_shapes=[pltpu.VMEM((tm, d), BF16), pltpu.VMEM((tm, d_ff), BF16)],
        compiler_params=_params(("parallel",)),
        name="ffn" + ("_proj" if proj is not None else "") + ("_final" if final_g is not None else ""),
    )(*args)


def _gdn_in_kernel(x_ref, g_ref, wqkv_ref, wgate_ref, wab_ref, cw_ref, alog_ref, dtb_ref,
                   q_ref, k_ref, v_ref, gate_ref, gb_ref, xn_ref, pbuf_ref,
                   *, tm, tiles_per_seq, d_qk):
    i = pl.program_id(0)
    xn_ref[...] = _rms(x_ref[...], g_ref[...]).astype(BF16)

    @pl.when(i % tiles_per_seq == 0)
    def _():
        pbuf_ref[0:SUBLANES, :] = jnp.zeros((SUBLANES, pbuf_ref.shape[1]), F32)

    outs = (q_ref, k_ref, v_ref)
    heads_per_piece = PROJ_COLS // GDN_DK
    pieces = [(part, g) for part in range(3) for g in range(d_qk // PROJ_COLS)]

    def project(part, g):
        cols = slice(part * d_qk + g * PROJ_COLS, part * d_qk + (g + 1) * PROJ_COLS)
        pbuf_ref[SUBLANES:SUBLANES + tm, cols] = _dot(xn_ref[...], wqkv_ref[:, cols])

    def conv_act_norm(part, g):
        scale = GDN_DK ** -0.5 if part == 0 else 1.0
        for h in range(g * heads_per_piece, (g + 1) * heads_per_piece):
            hc = slice(h * GDN_DK, (h + 1) * GDN_DK)
            cols = slice(part * d_qk + h * GDN_DK, part * d_qk + (h + 1) * GDN_DK)
            for r0 in range(0, tm, CONV_ROWS):
                y = cw_ref[GDN_CONV - 1:GDN_CONV, cols] * pbuf_ref[SUBLANES + r0:SUBLANES + r0 + CONV_ROWS, cols]
                for j in range(GDN_CONV - 1):
                    off = SUBLANES - (GDN_CONV - 1) + j + r0
                    y = y + cw_ref[j:j + 1, cols] * pbuf_ref[off:off + CONV_ROWS, cols]
                y = _silu(y)
                if part < 2:
                    ss = jnp.sum(y * y, axis=-1, keepdims=True)
                    y = y * (lax.rsqrt(ss + NORM_EPS) * scale)
                outs[part][r0:r0 + CONV_ROWS, hc] = y.astype(BF16)

    def gate(g):
        cols = slice(g * PROJ_COLS, (g + 1) * PROJ_COLS)
        gate_ref[:, cols] = _dot(xn_ref[...], wgate_ref[:, cols]).astype(gate_ref.dtype)

    n_gate = d_qk // PROJ_COLS
    project(*pieces[0])
    for idx, piece in enumerate(pieces):
        if idx + 1 < len(pieces):
            project(*pieces[idx + 1])
        elif n_gate:
            gate(0)
        conv_act_norm(*piece)
        tail = idx - (len(pieces) - n_gate)
        if 0 <= tail < n_gate - 1:
            gate(tail + 1)
    pbuf_ref[0:SUBLANES, :] = pbuf_ref[tm:tm + SUBLANES, :]

    ab = _dot(xn_ref[...], wab_ref[...])
    x = ab + dtb_ref[...]
    softplus = jnp.maximum(x, 0.0) + jnp.log1p(jnp.exp(-jnp.abs(x)))
    decay = -jnp.exp(alog_ref[...]) * softplus
    lane = lax.broadcasted_iota(jnp.int32, ab.shape, 1)
    gb_ref[...] = jnp.where(lane < GDN_HEADS, decay, jax.nn.sigmoid(ab))


def _gdn_in(h, g, wqkv, wgate, wab, conv_w, alog_row, dtb_row, seq, tm=512):
    m, d = h.shape
    d_qk = wgate.shape[1]
    row = lambda n: pl.BlockSpec((tm, n), lambda i: (i, 0))
    return pl.pallas_call(
        functools.partial(_gdn_in_kernel, tm=tm, tiles_per_seq=seq // tm, d_qk=d_qk),
        grid=(m // tm,),
        in_specs=[row(d), _resident((1, d)), _resident(wqkv.shape), _resident(wgate.shape),
                  _resident(wab.shape), _resident(conv_w.shape), _resident((1, LANES)),
                  _resident((1, LANES))],
        out_specs=[row(d_qk), row(d_qk), row(d_qk), row(d_qk), row(LANES)],
        out_shape=[jax.ShapeDtypeStruct((m, d_qk), BF16)] * 4 + [jax.ShapeDtypeStruct((m, LANES), F32)],
        scratch_shapes=[pltpu.VMEM((tm, d), BF16), pltpu.VMEM((tm + SUBLANES, 3 * d_qk), F32)],
        compiler_params=_params(("arbitrary",)),
        name="gdn_in",
    )(h, g.reshape(1, d), wqkv, wgate, wab, conv_w, alog_row, dtb_row)


def _split3(x):
    hi = x.astype(BF16)
    r = x - hi.astype(F32)
    mid = r.astype(BF16)
    lo = (r - mid.astype(F32)).astype(BF16)
    return hi, mid, lo


def _delta_selectors():
    nb = INV_BLOCK
    pack = np.zeros((GDN_PAIRS * LANES, LANES), np.float32)
    for r in range(GDN_PAIRS * LANES):
        p, l = divmod(r, LANES)
        h, c = 2 * p + l // CHUNK, l % CHUNK
        pack[r, nb * h + c % nb] = 1.0
    bcast = np.zeros((LANES, (nb - 1) * LANES), np.float32)
    for j in range(nb - 1):
        for c in range(LANES):
            bcast[nb * (c // nb) + j, LANES * j + c] = 1.0
    unpack = np.zeros((LANES, GDN_HEADS * CHUNK), np.float32)
    for h in range(GDN_HEADS):
        for c in range(CHUNK):
            unpack[nb * h + c % nb, CHUNK * h + c] = 1.0
    return tuple(jnp.asarray(a, BF16) for a in (pack, bcast, unpack))


def _delta_kernel(q_ref, k_ref, v_ref, gate_ref, gb_ref, on_ref, pack_ref, bcast_ref, unpack_ref,
                  o_ref, state_ref, *, tc, ns):
    @pl.when(pl.program_id(1) == 0)
    def _():
        state_ref[...] = jnp.zeros(state_ref.shape, F32)

    nb = INV_BLOCK
    ri = lax.broadcasted_iota(jnp.int32, (CHUNK, LANES), 0)
    li = lax.broadcasted_iota(jnp.int32, (CHUNK, LANES), 1)
    ci = li % CHUNK
    lo64 = li < CHUNK
    lo64_row = lo64[0:1]
    causal = ri >= ci
    strict = ri > ci
    bd16 = (ri // nb) == (ci // nb)
    bd32 = (ri // (2 * nb)) == (ci // (2 * nb))
    off32 = jnp.logical_and(bd32, jnp.logical_not(bd16))
    off64 = jnp.logical_not(bd32)
    lo128 = lax.broadcasted_iota(jnp.int32, (LANES, LANES), 1) < CHUNK
    tril = jnp.where(lax.broadcasted_iota(jnp.int32, (CHUNK, CHUNK), 0)
                     >= lax.broadcasted_iota(jnp.int32, (CHUNK, CHUNK), 1), 1.0, 0.0).astype(BF16)
    sub_i = lax.broadcasted_iota(jnp.int32, (SUBLANES, LANES), 0)
    sub_l = lax.broadcasted_iota(jnp.int32, (SUBLANES, LANES), 1) % nb
    eye_parts = [jnp.where(sub_l == sub_i + SUBLANES * t, 1.0, 0.0).astype(F32) for t in range(2)]

    def bdiag64(x):
        z = jnp.zeros_like(x)
        return _cat0(jnp.where(lo64, x, z), jnp.where(lo64, z, x))

    def bdiag128(x):
        z = jnp.zeros_like(x)
        return _cat0(jnp.where(lo128, x, z), jnp.where(lo128, z, x))

    hc = lambda h: slice(h * GDN_DK, (h + 1) * GDN_DK)
    col = lambda x, h: x[:, h:h + 1]
    seqs = range(ns)
    pairs = range(GDN_PAIRS)

    def chunk(c, carry):
        rows = pl.ds(pl.multiple_of(c * CHUNK, CHUNK), CHUNK)

        gb, gcum, gcum_t = [], [], []
        for n in seqs:
            gb.append(gb_ref[n, rows, :])
            hi, mid, lo = _split3(gb[n])
            gcum.append(_dot(tril, hi) + _dot(tril, mid) + _dot(tril, lo))
            gcum_t.append(_cat0(gcum[n], gcum[n]).T)

        a_mats, attns, kdts, kbs, egs = {}, {}, {}, {}, {}
        for n in seqs:
            for p in pairs:
                ha, hb = 2 * p, 2 * p + 1
                gc_col = jnp.where(lo64, col(gcum[n], ha), col(gcum[n], hb))
                gc_row = jnp.where(lo64_row, gcum_t[n][ha:ha + 1, :], gcum_t[n][hb:hb + 1, :])
                g_last = jnp.where(lo64_row, gcum[n][CHUNK - 1:CHUNK, ha:ha + 1],
                                   gcum[n][CHUNK - 1:CHUNK, hb:hb + 1])
                decay = jnp.where(causal, jnp.exp(jnp.where(causal, gc_col - gc_row, 0.0)), 0.0)
                k_a = k_ref[n, rows, hc(ha)].astype(F32)
                k_b = k_ref[n, rows, hc(hb)].astype(F32)
                kt = _cat0(k_a, k_b).T
                kb_a = k_a * col(gb[n], GDN_HEADS + ha)
                kb_b = k_b * col(gb[n], GDN_HEADS + hb)
                lhs = _cat0(_cat1(kb_a, kb_b).astype(BF16),
                            _cat1(q_ref[n, rows, hc(ha)], q_ref[n, rows, hc(hb)]))
                kq = _dot(lhs, bdiag128(kt.astype(BF16)))
                a_mats[n, p] = jnp.where(strict, kq[:CHUNK] * decay, 0.0)
                attns[n, p] = (kq[CHUNK:] * decay).astype(BF16)
                kdts[n, p] = (kt * jnp.exp(g_last - gc_row)).astype(BF16)
                kbs[n, p] = (kb_a, kb_b)
                egs[n, p] = (jnp.exp(col(gcum[n], ha)), jnp.exp(col(gcum[n], hb)))

        cb, x = [], []
        for n in seqs:
            ad_pk = _dot(_cat1(*[jnp.where(bd16, a_mats[n, p], 0.0) for p in pairs]).astype(BF16), pack_ref[...])
            cb.append(_dot(ad_pk.astype(BF16), bcast_ref[...]))
            x.append([[eye_parts[0], eye_parts[1]] for _ in range(CHUNK // nb)])
        for j in range(nb - 1):
            tj, rj = divmod(j, SUBLANES)
            for n in seqs:
                cbj = cb[n][:, j * LANES:(j + 1) * LANES]
                for b in range(CHUNK // nb):
                    r = jnp.broadcast_to(x[n][b][tj][rj:rj + 1, :], (SUBLANES, LANES))
                    if j < SUBLANES - 1:
                        x[n][b][0] = x[n][b][0] - cbj[nb * b:nb * b + SUBLANES] * r
                    x[n][b][1] = x[n][b][1] - cbj[nb * b + SUBLANES:nb * (b + 1)] * r
        x_un = [_dot(_cat0(*[part for blk in x[n] for part in blk]).astype(BF16), unpack_ref[...]) for n in seqs]

        keys = [(n, p) for n in seqs for p in pairs]
        xd = {k: jnp.where(bd16, x_un[k[0]][:, k[1] * LANES:(k[1] + 1) * LANES], 0.0) for k in keys}
        m1 = {k: _dot(xd[k].astype(BF16), bdiag64(jnp.where(off32, a_mats[k], 0.0).astype(BF16))) for k in keys}
        x32 = {k: xd[k] - _dot(m1[k].astype(BF16), bdiag64(xd[k].astype(BF16))) for k in keys}
        m2 = {k: _dot(x32[k].astype(BF16), bdiag64(jnp.where(off64, a_mats[k], 0.0).astype(BF16))) for k in keys}
        tinv = {k: (x32[k] - _dot(m2[k].astype(BF16), bdiag64(x32[k].astype(BF16)))).astype(BF16) for k in keys}

        yws = {}
        for n, p in keys:
            ha, hb = 2 * p, 2 * p + 1
            rhs_a = _cat1(v_ref[n, rows, hc(ha)].astype(F32) * col(gb[n], GDN_HEADS + ha),
                          kbs[n, p][0] * egs[n, p][0]).astype(BF16)
            rhs_b = _cat1(v_ref[n, rows, hc(hb)].astype(F32) * col(gb[n], GDN_HEADS + hb),
                          kbs[n, p][1] * egs[n, p][1]).astype(BF16)
            z = jnp.zeros_like(rhs_a)
            yws[n, p] = _dot(tinv[n, p], _cat0(_cat1(rhs_a, z), _cat1(z, rhs_b)))

        wq = {}
        for n, p in keys:
            ha, hb = 2 * p, 2 * p + 1
            yw = yws[n, p]
            sb = state_ref[n, p].astype(BF16)
            z = jnp.zeros((GDN_DK, GDN_DK), BF16)
            s_bd = _cat0(_cat1(sb[:GDN_DK], z), _cat1(z, sb[GDN_DK:]))
            qd_a = q_ref[n, rows, hc(ha)].astype(F32) * egs[n, p][0]
            qd_b = q_ref[n, rows, hc(hb)].astype(F32) * egs[n, p][1]
            lhs = _cat0(_cat1(yw[:, 1 * GDN_DK:2 * GDN_DK], yw[:, 3 * GDN_DK:4 * GDN_DK]),
                        _cat1(qd_a, qd_b)).astype(BF16)
            wq[n, p] = _dot(lhs, s_bd)
        outs, upds = {}, {}
        for n, p in keys:
            yw = yws[n, p]
            v_new = _cat1(yw[:, 0:GDN_DK], yw[:, 2 * GDN_DK:3 * GDN_DK]) - wq[n, p][:CHUNK]
            vnb = v_new.astype(BF16)
            zc = jnp.zeros((CHUNK, GDN_DK), BF16)
            vn_bd = _cat0(_cat1(vnb[:, :GDN_DK], zc), _cat1(zc, vnb[:, GDN_DK:]))
            outs[n, p] = wq[n, p][CHUNK:] + _dot(attns[n, p], vn_bd)
            upds[n, p] = _dot(bdiag128(kdts[n, p]), _cat0(vnb[:, :GDN_DK], vnb[:, GDN_DK:]))
        for n, p in keys:
            ha, hb = 2 * p, 2 * p + 1
            e_a = jnp.exp(gcum[n][CHUNK - 1:CHUNK, ha:ha + 1])
            e_b = jnp.exp(gcum[n][CHUNK - 1:CHUNK, hb:hb + 1])
            state_ref[n, p, 0:GDN_DK, :] = state_ref[n, p, 0:GDN_DK, :] * e_a + upds[n, p][:GDN_DK]
            state_ref[n, p, GDN_DK:2 * GDN_DK, :] = (state_ref[n, p, GDN_DK:2 * GDN_DK, :] * e_b
                                                     + upds[n, p][GDN_DK:])
            for s_idx, h in enumerate((ha, hb)):
                o_h = outs[n, p][:, s_idx * GDN_DK:(s_idx + 1) * GDN_DK]
                on = _rms(o_h, on_ref[...]) * _silu(gate_ref[n, rows, hc(h)].astype(F32))
                o_ref[n, rows, hc(h)] = on.astype(BF16)
        return carry

    lax.fori_loop(0, tc // CHUNK, chunk, 0)


def _delta(q, k, v, gate, gb, out_norm, batch, seq, tc=256):
    m, d = q.shape
    ns = max(n for n in (4, 2, 1) if batch % n == 0)
    sel = _delta_selectors()
    row = lambda n: pl.BlockSpec((ns, tc, n), lambda b, t: (b, t, 0))
    as3d = lambda a: a.reshape(batch, seq, a.shape[-1])
    out = pl.pallas_call(
        functools.partial(_delta_kernel, tc=tc, ns=ns),
        grid=(batch // ns, seq // tc),
        in_specs=[row(d), row(d), row(d), row(d), row(LANES), _resident((1, GDN_DK))]
        + [_resident(s.shape) for s in sel],
        out_specs=row(d),
        out_shape=jax.ShapeDtypeStruct((batch, seq, d), BF16),
        scratch_shapes=[pltpu.VMEM((ns, GDN_PAIRS, 2 * GDN_DK, GDN_DK), F32)],
        compiler_params=_params(("parallel", "arbitrary")),
        name="gdn_delta",
    )(as3d(q), as3d(k), as3d(v), as3d(gate), as3d(gb), out_norm.reshape(1, GDN_DK), *sel)
    return out.reshape(m, d)


def _rope_angles(pos_ref, inv_ref):
    ang = pos_ref[...].astype(F32) * inv_ref[...]
    return jnp.cos(ang), jnp.sin(ang)


def _mla_kv_kernel(x_ref, g_ref, wc_ref, wr_ref, wrr_ref, g2_ref, wkb_ref, wvbt_ref,
                   pos_ref, inv_ref, k_ref, vt_ref):
    xn = _rms(x_ref[...], g_ref[...]).astype(BF16)
    cn = _rms(_dot(xn, wc_ref[...]), g2_ref[...]).astype(BF16)
    cos, sin = _rope_angles(pos_ref, inv_ref)
    kr = _dot(xn, wr_ref[...]) * cos + _dot(xn, wrr_ref[...]) * sin
    low = lax.broadcasted_iota(jnp.int32, kr.shape, 1) < MLA_ROPE
    kr_halves = (jnp.where(low, kr, 0.0).astype(BF16), jnp.where(low, 0.0, kr).astype(BF16))
    kn = _dot(cn, wkb_ref[...]).astype(BF16)
    for h in range(MLA_HEADS):
        k_ref[:, h * MLA_QK:h * MLA_QK + MLA_NOPE] = kn[:, h * MLA_NOPE:(h + 1) * MLA_NOPE]
        k_ref[:, h * MLA_QK + MLA_NOPE:(h + 1) * MLA_QK] = kr_halves[h % 2]
    vt = _dot_nt(wvbt_ref[...], cn).astype(BF16)
    ones = jnp.ones((MLA_VX - MLA_V, vt.shape[1]), BF16)
    for h in range(MLA_HEADS):
        vt_ref[h * MLA_VX:h * MLA_VX + MLA_V, :] = vt[h * MLA_V:(h + 1) * MLA_V]
        vt_ref[h * MLA_VX + MLA_V:(h + 1) * MLA_VX, :] = ones


def _mla_kv(h, g, wc, wr, wrr, g2, wkb, wvbt, pos, inv, batch, seq, tm=512):
    m, d = h.shape
    n = wkb.shape[1]
    tps = seq // tm
    row = lambda w: pl.BlockSpec((tm, w), lambda i: (i, 0))
    return pl.pallas_call(
        _mla_kv_kernel,
        grid=(m // tm,),
        in_specs=[row(d), _resident((1, d)), _resident(wc.shape), _resident(wr.shape),
                  _resident(wrr.shape), _resident((1, wc.shape[1])), _resident(wkb.shape),
                  _resident(wvbt.shape), row(1), _resident((1, LANES))],
        out_specs=[row(MLA_HEADS * MLA_QK),
                   pl.BlockSpec((None, MLA_HEADS * MLA_VX, tm), lambda i: (i // tps, 0, i % tps))],
        out_shape=[jax.ShapeDtypeStruct((m, MLA_HEADS * MLA_QK), BF16),
                   jax.ShapeDtypeStruct((batch, MLA_HEADS * MLA_VX, seq), BF16)],
        compiler_params=_params(("parallel",)),
        name="mla_kv",
    )(h, g.reshape(1, d), wc, wr, wrr, g2.reshape(1, -1), wkb, wvbt, pos, inv)


def _mla_q_kernel(x_ref, g_ref, wdq_ref, g2_ref, wqnt_ref, wqrt_ref, wqrrt_ref, pos_ref, inv_ref,
                  qt_ref, *, scale):
    xn = _rms(x_ref[...], g_ref[...]).astype(BF16)
    ql = _rms(_dot(xn, wdq_ref[...]), g2_ref[...]).astype(BF16)
    qnt = (_dot_nt(wqnt_ref[...], ql) * scale).astype(BF16)
    half = MLA_ROPE // 2
    ang = inv_ref[0:half, :] * pos_ref[...].astype(F32)
    cos = _cat0(*[jnp.cos(ang) * scale] * (LANES // half))
    sin = _cat0(*[jnp.sin(ang) * scale] * (LANES // half))
    qrt = _dot_nt(wqrt_ref[...], ql)
    qrrt = _dot_nt(wqrrt_ref[...], ql)
    low = lax.broadcasted_iota(jnp.int32, cos.shape, 0) < MLA_ROPE
    for p in range(MLA_HEADS // 2):
        pr = slice(p * LANES, (p + 1) * LANES)
        r = qrt[pr] * cos + qrrt[pr] * sin
        halves = (jnp.where(low, r, 0.0).astype(BF16), jnp.where(low, 0.0, r).astype(BF16))
        for s in range(2):
            h = 2 * p + s
            qt_ref[h * MLA_QK:h * MLA_QK + MLA_NOPE, :] = qnt[h * MLA_NOPE:(h + 1) * MLA_NOPE]
            qt_ref[h * MLA_QK + MLA_NOPE:(h + 1) * MLA_QK, :] = halves[s]


def _mla_q(h, g, wdq, g2, wqnt, wqrt, wqrrt, pos_rows, inv_col, scale, batch, seq, tm=Q_TILE):
    m, d = h.shape
    tps = seq // tm
    row = lambda w: pl.BlockSpec((tm, w), lambda i: (i, 0))
    return pl.pallas_call(
        functools.partial(_mla_q_kernel, scale=scale),
        grid=(m // tm,),
        in_specs=[row(d), _resident((1, d)), _resident(wdq.shape), _resident((1, wdq.shape[1])),
                  _resident(wqnt.shape), _resident(wqrt.shape), _resident(wqrrt.shape),
                  pl.BlockSpec((None, 1, tm), lambda i: (i, 0, 0)), _resident((LANES, 1))],
        out_specs=pl.BlockSpec((None, MLA_HEADS * MLA_QK, tm), lambda i: (i // tps, 0, i % tps)),
        out_shape=jax.ShapeDtypeStruct((batch, MLA_HEADS * MLA_QK, seq), BF16),
        compiler_params=_params(("parallel",)),
        name="mla_q",
    )(h, g.reshape(1, d), wdq, g2.reshape(1, -1), wqnt, wqrt, wqrrt, pos_rows, inv_col)


def _attn_kernel(qt_ref, k_ref, vt_ref, o_ref, m_ref, acc_ref, *, t):
    qi = pl.program_id(1)
    key_chunk = lax.broadcasted_iota(jnp.int32, (t, t), 0) // CHUNK
    qry_chunk = lax.broadcasted_iota(jnp.int32, (t, t), 1) // CHUNK
    diag_mask = qry_chunk >= key_chunk

    m_ref[...] = jnp.full(m_ref.shape, -jnp.inf, F32)
    acc_ref[...] = jnp.zeros(acc_ref.shape, F32)

    def scores(j, h):
        rows = pl.ds(pl.multiple_of(j * t, t), t)
        hq = slice(h * MLA_QK, (h + 1) * MLA_QK)
        return _dot(k_ref[rows, hq], qt_ref[hq, :])

    def step(j, pending, masked):
        rows = pl.ds(pl.multiple_of(j * t, t), t)
        pending = list(pending)
        for h in range(MLA_HEADS):
            hv = slice(h * MLA_VX, (h + 1) * MLA_VX)
            s = pending.pop(0)
            ahead = h + SCORE_LOOKAHEAD
            if ahead < MLA_HEADS:
                pending.append(scores(j, ahead))
            elif not masked:
                pending.append(scores(j + 1, ahead - MLA_HEADS))
            if masked:
                s = jnp.where(diag_mask, s, -jnp.inf)
            m_old = m_ref[h]
            m_new = jnp.maximum(m_old, jnp.max(s, axis=0, keepdims=True))
            alpha = jnp.exp2(m_old - m_new)
            p = jnp.exp2(s - m_new)
            acc_ref[h] = alpha * acc_ref[h] + _dot(vt_ref[hv, rows], p.astype(BF16))
            m_ref[h] = m_new
        return tuple(pending)

    pending = tuple(scores(0, h) for h in range(SCORE_LOOKAHEAD))
    pending = lax.fori_loop(0, qi, lambda j, c: step(j, c, False), pending)
    step(qi, pending, True)
    for h in range(MLA_HEADS):
        acc = acc_ref[h]
        o_ref[:, h * MLA_V:(h + 1) * MLA_V] = (acc[:MLA_V] / acc[MLA_V:MLA_V + 1]).T.astype(BF16)


def _attn(qt, k, vt, batch, seq, t=256):
    m, dq = k.shape
    dv = MLA_HEADS * MLA_V
    nq = seq // t
    return pl.pallas_call(
        functools.partial(_attn_kernel, t=t),
        grid=(batch, nq),
        in_specs=[pl.BlockSpec((None, dq, t), lambda b, i: (b, 0, i)),
                  pl.BlockSpec((seq, dq), lambda b, i: (b, 0)),
                  pl.BlockSpec((None, MLA_HEADS * MLA_VX, seq), lambda b, i: (b, 0, 0))],
        out_specs=pl.BlockSpec((t, dv), lambda b, i: (b * nq + i, 0)),
        out_shape=jax.ShapeDtypeStruct((m, dv), BF16),
        scratch_shapes=[pltpu.VMEM((MLA_HEADS, 1, t), F32), pltpu.VMEM((MLA_HEADS, MLA_VX, t), F32)],
        compiler_params=_params(("parallel", "arbitrary")),
        name="mla_attn",
    )(qt, k, vt)


def _rot_half_cols(w, heads):
    k = w.shape[0]
    w = w.reshape(k, heads, 2, MLA_ROPE // 2)
    return jnp.stack([-w[:, :, 1], w[:, :, 0]], axis=2).reshape(k, heads * MLA_ROPE)


def kernel(x, positions, ffn1_norm, ffn1_w_gu, ffn1_w_down, mix_norm, ffn2_norm, ffn2_w_gu, ffn2_w_down, gdn_w_in, gdn_conv_w, gdn_a_log, gdn_dt_bias, gdn_out_norm, gdn_w_out, kv_norm, mla_w_kv_a, mla_kv_a_norm, mla_w_kv_b, mla_w_dq, mla_q_norm, mla_w_uq, mla_w_o, final_norm):
    batch, seq, d = x.shape
    m = batch * seq
    depth = ffn1_norm.shape[0]
    n_a = gdn_w_in.shape[0]
    d_qk = GDN_HEADS * GDN_DK
    bf = lambda w: w.astype(BF16)

    h = x.reshape(m, d)
    pos = positions.reshape(m, 1)
    pos_rows = positions.reshape(m // Q_TILE, 1, Q_TILE)
    half = MLA_ROPE // 2
    inv = ROPE_THETA ** (-jnp.arange(half, dtype=F32) / half)
    inv = jnp.tile(inv, LANES // half).reshape(1, LANES)
    scale = (MLA_NOPE + MLA_ROPE) ** -0.5 * LOG2_E

    kk = vt = None
    for layer in range(depth):
        h = _ffn(h, ffn1_norm[layer], bf(ffn1_w_gu[layer]), bf(ffn1_w_down[layer]))
        if layer < n_a:
            i = layer
            w_in = gdn_w_in[i]
            wab = jnp.pad(w_in[:, 4 * d_qk:], ((0, 0), (0, LANES - 2 * GDN_HEADS)))
            pad_row = lambda p: jnp.pad(p.astype(F32), (0, LANES - GDN_HEADS)).reshape(1, LANES)
            q, k, v, gate, gb = _gdn_in(
                h, mix_norm[layer], bf(w_in[:, :3 * d_qk]), bf(w_in[:, 3 * d_qk:4 * d_qk]), bf(wab),
                gdn_conv_w[i], pad_row(gdn_a_log[i]), pad_row(gdn_dt_bias[i]), seq)
            mixed = _delta(q, k, v, gate, gb, gdn_out_norm[i], batch, seq)
            w_proj = bf(gdn_w_out[i])
        else:
            j = layer - n_a
            w_uq = mla_w_uq[j].reshape(-1, MLA_HEADS, MLA_NOPE + MLA_ROPE)
            wqn = w_uq[:, :, :MLA_NOPE].reshape(-1, MLA_HEADS * MLA_NOPE)
            wqr = w_uq[:, :, MLA_NOPE:].reshape(-1, MLA_HEADS * MLA_ROPE)
            qt = _mla_q(h, mix_norm[layer], bf(mla_w_dq[j]), mla_q_norm[j], bf(wqn.T), bf(wqr.T),
                        bf(_rot_half_cols(wqr, MLA_HEADS).T), pos_rows, inv.reshape(LANES, 1), scale,
                        batch, seq)
            mixed = _attn(qt, kk, vt, batch, seq)
            w_proj = bf(mla_w_o[j])
        last = layer == depth - 1
        h = _ffn(h, ffn2_norm[layer], bf(ffn2_w_gu[layer]), bf(ffn2_w_down[layer]),
                 proj=(mixed, w_proj), final_g=final_norm if last else None)
        if layer == n_a - 1:
            wr = mla_w_kv_a[:, MLA_KV_RANK:]
            w_kv_b = mla_w_kv_b.reshape(MLA_KV_RANK, MLA_HEADS, MLA_NOPE + MLA_V)
            twice = lambda w: jnp.concatenate([w, w], axis=1)
            kk, vt = _mla_kv(
                h, kv_norm, bf(mla_w_kv_a[:, :MLA_KV_RANK]), bf(twice(wr)),
                bf(twice(_rot_half_cols(wr, 1))), mla_kv_a_norm,
                bf(w_kv_b[:, :, :MLA_NOPE].reshape(MLA_KV_RANK, -1)),
                bf(w_kv_b[:, :, MLA_NOPE:].reshape(MLA_KV_RANK, -1).T), pos, inv, batch, seq)
    return h.reshape(batch, seq, d)
```

```python
import functools

import jax
import jax.numpy as jnp
import numpy as np
from jax import lax
from jax.experimental import pallas as pl
from jax.experimental.pallas import tpu as pltpu

F32 = jnp.float32
BF16 = jnp.bfloat16

NORM_EPS = 1e-6
CHUNK = 64
GDN_HEADS = 8
GDN_PAIRS = GDN_HEADS // 2
GDN_DK = 128
GDN_CONV = 4
MLA_HEADS = 8
MLA_NOPE = 128
MLA_ROPE = 64
MLA_V = 128
MLA_VX = MLA_V + 16
MLA_KV_RANK = 256
MLA_QK = 256
LOG2_E = 1.4426950408889634
ROPE_THETA = 10000.0
LANES = 128
SUBLANES = 8
INV_BLOCK = 16
SCORE_LOOKAHEAD = 3
Q_TILE = 512
ATTN_TILE = 256
FFN_TILE = 1024
CONV_ROWS = 128
PROJ_COLS = 256
VMEM_LIMIT = 56 * 1024 * 1024


def _resident(shape):
    nd = len(shape)
    return pl.BlockSpec(shape, lambda *_: (0,) * nd, pipeline_mode=pl.Buffered(1))


def _resident_layer(stacked_shape, layer):
    nd = len(stacked_shape) - 1
    return pl.BlockSpec((None,) + tuple(stacked_shape[1:]), lambda *_: (layer,) + (0,) * nd,
                        pipeline_mode=pl.Buffered(1))


def _params(sem):
    return pltpu.CompilerParams(dimension_semantics=sem, vmem_limit_bytes=VMEM_LIMIT)


def _rms(x, g):
    ms = jnp.mean(x * x, axis=-1, keepdims=True)
    return x * lax.rsqrt(ms + NORM_EPS) * g


def _silu(x):
    return x * jax.nn.sigmoid(x)


def _dot(a, b):
    return jnp.dot(a, b, preferred_element_type=F32)


def _dot_nt(a, b):
    return lax.dot_general(a, b, (((1,), (1,)), ((), ())), preferred_element_type=F32)


def _cat0(*xs):
    return jnp.concatenate(xs, axis=0)


def _cat1(*xs):
    return jnp.concatenate(xs, axis=1)


def _ffn_kernel(*refs, d_ff, tf, proj, final):
    refs = list(refs)
    x_ref = refs.pop(0)
    if proj:
        a_ref, wp_ref = refs.pop(0), refs.pop(0)
    g_ref, wgu_ref, wd_ref = refs.pop(0), refs.pop(0), refs.pop(0)
    if final:
        gf_ref = refs.pop(0)
    o_ref, xn_ref, act_ref = refs
    if proj:
        o_ref[...] = x_ref[...] + _dot(a_ref[...], wp_ref[...])
        h_ref = o_ref
    else:
        h_ref = x_ref
    xn_ref[...] = _rms(h_ref[...], g_ref[...]).astype(BF16)
    for c in range(d_ff // tf):
        g = _dot(xn_ref[...], wgu_ref[:, c * tf:(c + 1) * tf])
        u = _dot(xn_ref[...], wgu_ref[:, d_ff + c * tf:d_ff + (c + 1) * tf])
        act_ref[:, c * tf:(c + 1) * tf] = (_silu(g) * u).astype(BF16)
    y = h_ref[...] + 0.5 * _dot(act_ref[...], wd_ref[...])
    if final:
        y = _rms(y, gf_ref[...])
    o_ref[...] = y


def _ffn(h, g, wgu, wd, layer, proj=None, final_g=None, tm=FFN_TILE, tf=256):
    m, d = h.shape
    d_ff = wd.shape[1]
    row = lambda n: pl.BlockSpec((tm, n), lambda i: (i, 0))
    in_specs, args = [row(d)], [h]
    if proj is not None:
        a, wp = proj
        in_specs += [row(a.shape[1]), _resident(wp.shape)]
        args += [a, wp]
    in_specs += [_resident((1, d)), _resident_layer(wgu.shape, layer), _resident_layer(wd.shape, layer)]
    args += [g.reshape(1, d), wgu, wd]
    if final_g is not None:
        in_specs.append(_resident((1, d)))
        args.append(final_g.reshape(1, d))
    return pl.pallas_call(
        functools.partial(_ffn_kernel, d_ff=d_ff, tf=tf, proj=proj is not None,
                          final=final_g is not None),
        grid=(m // tm,),
        in_specs=in_specs,
        out_specs=row(d),
        out_shape=jax.ShapeDtypeStruct((m, d), F32),
        scratch_shapes=[pltpu.VMEM((tm, d), BF16), pltpu.VMEM((tm, d_ff), BF16)],
        compiler_params=_params(("parallel",)),
        name="ffn" + ("_proj" if proj is not None else "") + ("_final" if final_g is not None else ""),
    )(*args)


def _gdn_in_kernel(x_ref, g_ref, wqkv_ref, wgate_ref, wab_ref, cw_ref, alog_ref, dtb_ref,
                   q_ref, k_ref, v_ref, gate_ref, gb_ref, xn_ref, pbuf_ref,
                   *, tm, tiles_per_seq, d_qk):
    i = pl.program_id(0)
    xn_ref[...] = _rms(x_ref[...], g_ref[...]).astype(BF16)

    @pl.when(i % tiles_per_seq == 0)
    def _():
        pbuf_ref[0:SUBLANES, :] = jnp.zeros((SUBLANES, pbuf_ref.shape[1]), F32)

    outs = (q_ref, k_ref, v_ref)
    heads_per_piece = PROJ_COLS // GDN_DK
    pieces = [(part, g) for part in range(3) for g in range(d_qk // PROJ_COLS)]

    def project(part, g):
        cols = slice(part * d_qk + g * PROJ_COLS, part * d_qk + (g + 1) * PROJ_COLS)
        pbuf_ref[SUBLANES:SUBLANES + tm, cols] = _dot(xn_ref[...], wqkv_ref[:, cols])

    def conv_act_norm(part, g):
        scale = GDN_DK ** -0.5 if part == 0 else 1.0
        for h in range(g * heads_per_piece, (g + 1) * heads_per_piece):
            hc = slice(h * GDN_DK, (h + 1) * GDN_DK)
            cols = slice(part * d_qk + h * GDN_DK, part * d_qk + (h + 1) * GDN_DK)
            for r0 in range(0, tm, CONV_ROWS):
                y = cw_ref[GDN_CONV - 1:GDN_CONV, cols] * pbuf_ref[SUBLANES + r0:SUBLANES + r0 + CONV_ROWS, cols]
                for j in range(GDN_CONV - 1):
                    off = SUBLANES - (GDN_CONV - 1) + j + r0
                    y = y + cw_ref[j:j + 1, cols] * pbuf_ref[off:off + CONV_ROWS, cols]
                y = _silu(y)
                if part < 2:
                    ss = jnp.sum(y * y, axis=-1, keepdims=True)
                    y = y * (lax.rsqrt(ss + NORM_EPS) * scale)
                outs[part][r0:r0 + CONV_ROWS, hc] = y.astype(BF16)

    def gate(g):
        cols = slice(g * PROJ_COLS, (g + 1) * PROJ_COLS)
        gate_ref[:, cols] = _dot(xn_ref[...], wgate_ref[:, cols]).astype(gate_ref.dtype)

    n_gate = d_qk // PROJ_COLS
    project(*pieces[0])
    for idx, piece in enumerate(pieces):
        if idx + 1 < len(pieces):
            project(*pieces[idx + 1])
        elif n_gate:
            gate(0)
        conv_act_norm(*piece)
        tail = idx - (len(pieces) - n_gate)
        if 0 <= tail < n_gate - 1:
            gate(tail + 1)
    pbuf_ref[0:SUBLANES, :] = pbuf_ref[tm:tm + SUBLANES, :]

    ab = _dot(xn_ref[...], wab_ref[...])
    x = ab + dtb_ref[...]
    softplus = jnp.maximum(x, 0.0) + jnp.log1p(jnp.exp(-jnp.abs(x)))
    decay = -jnp.exp(alog_ref[...]) * softplus
    lane = lax.broadcasted_iota(jnp.int32, ab.shape, 1)
    gb_ref[...] = jnp.where(lane < GDN_HEADS, decay, jax.nn.sigmoid(ab))


def _gdn_in(h, g, wqkv, wgate, wab, conv_w, alog_row, dtb_row, seq, tm=512):
    m, d = h.shape
    d_qk = wgate.shape[1]
    row = lambda n: pl.BlockSpec((tm, n), lambda i: (i, 0))
    return pl.pallas_call(
        functools.partial(_gdn_in_kernel, tm=tm, tiles_per_seq=seq // tm, d_qk=d_qk),
        grid=(m // tm,),
        in_specs=[row(d), _resident((1, d)), _resident(wqkv.shape), _resident(wgate.shape),
                  _resident(wab.shape), _resident(conv_w.shape), _resident((1, LANES)),
                  _resident((1, LANES))],
        out_specs=[row(d_qk), row(d_qk), row(d_qk), row(d_qk), row(LANES)],
        out_shape=[jax.ShapeDtypeStruct((m, d_qk), BF16)] * 4 + [jax.ShapeDtypeStruct((m, LANES), F32)],
        scratch_shapes=[pltpu.VMEM((tm, d), BF16), pltpu.VMEM((tm + SUBLANES, 3 * d_qk), F32)],
        compiler_params=_params(("arbitrary",)),
        name="gdn_in",
    )(h, g.reshape(1, d), wqkv, wgate, wab, conv_w, alog_row, dtb_row)


def _split3(x):
    hi = x.astype(BF16)
    r = x - hi.astype(F32)
    mid = r.astype(BF16)
    lo = (r - mid.astype(F32)).astype(BF16)
    return hi, mid, lo


def _delta_selectors():
    nb = INV_BLOCK
    pack = np.zeros((GDN_PAIRS * LANES, LANES), np.float32)
    for r in range(GDN_PAIRS * LANES):
        p, l = divmod(r, LANES)
        h, c = 2 * p + l // CHUNK, l % CHUNK
        pack[r, nb * h + c % nb] = 1.0
    bcast = np.zeros((LANES, (nb - 1) * LANES), np.float32)
    for j in range(nb - 1):
        for c in range(LANES):
            bcast[nb * (c // nb) + j, LANES * j + c] = 1.0
    unpack = np.zeros((LANES, GDN_HEADS * CHUNK), np.float32)
    for h in range(GDN_HEADS):
        for c in range(CHUNK):
            unpack[nb * h + c % nb, CHUNK * h + c] = 1.0
    return tuple(jnp.asarray(a, BF16) for a in (pack, bcast, unpack))


def _delta_kernel(q_ref, k_ref, v_ref, gate_ref, gb_ref, on_ref, pack_ref, bcast_ref, unpack_ref,
                  o_ref, state_ref, *, tc, ns):
    @pl.when(pl.program_id(1) == 0)
    def _():
        state_ref[...] = jnp.zeros(state_ref.shape, F32)

    nb = INV_BLOCK
    ri = lax.broadcasted_iota(jnp.int32, (CHUNK, LANES), 0)
    li = lax.broadcasted_iota(jnp.int32, (CHUNK, LANES), 1)
    ci = li % CHUNK
    lo64 = li < CHUNK
    lo64_row = lo64[0:1]
    causal = ri >= ci
    strict = ri > ci
    bd16 = (ri // nb) == (ci // nb)
    bd32 = (ri // (2 * nb)) == (ci // (2 * nb))
    off32 = jnp.logical_and(bd32, jnp.logical_not(bd16))
    off64 = jnp.logical_not(bd32)
    lo128 = lax.broadcasted_iota(jnp.int32, (LANES, LANES), 1) < CHUNK
    tril = jnp.where(lax.broadcasted_iota(jnp.int32, (CHUNK, CHUNK), 0)
                     >= lax.broadcasted_iota(jnp.int32, (CHUNK, CHUNK), 1), 1.0, 0.0).astype(BF16)
    sub_i = lax.broadcasted_iota(jnp.int32, (SUBLANES, LANES), 0)
    sub_l = lax.broadcasted_iota(jnp.int32, (SUBLANES, LANES), 1) % nb
    eye_parts = [jnp.where(sub_l == sub_i + SUBLANES * t, 1.0, 0.0).astype(F32) for t in range(2)]

    def bdiag64(x):
        z = jnp.zeros_like(x)
        return _cat0(jnp.where(lo64, x, z), jnp.where(lo64, z, x))

    def bdiag128(x):
        z = jnp.zeros_like(x)
        return _cat0(jnp.where(lo128, x, z), jnp.where(lo128, z, x))

    hc = lambda h: slice(h * GDN_DK, (h + 1) * GDN_DK)
    col = lambda x, h: x[:, h:h + 1]
    seqs = range(ns)
    pairs = range(GDN_PAIRS)

    def chunk(c, carry):
        rows = pl.ds(pl.multiple_of(c * CHUNK, CHUNK), CHUNK)

        gb, gcum, gcum_t = [], [], []
        for n in seqs:
            gb.append(gb_ref[n, rows, :])
            hi, mid, lo = _split3(gb[n])
            gcum.append(_dot(tril, hi) + _dot(tril, mid) + _dot(tril, lo))
            gcum_t.append(_cat0(gcum[n], gcum[n]).T)

        a_mats, attns, kdts, kbs, egs = {}, {}, {}, {}, {}
        for n in seqs:
            for p in pairs:
                ha, hb = 2 * p, 2 * p + 1
                gc_col = jnp.where(lo64, col(gcum[n], ha), col(gcum[n], hb))
                gc_row = jnp.where(lo64_row, gcum_t[n][ha:ha + 1, :], gcum_t[n][hb:hb + 1, :])
                g_last = jnp.where(lo64_row, gcum[n][CHUNK - 1:CHUNK, ha:ha + 1],
                                   gcum[n][CHUNK - 1:CHUNK, hb:hb + 1])
                decay = jnp.where(causal, jnp.exp(jnp.where(causal, gc_col - gc_row, 0.0)), 0.0)
                k_a = k_ref[n, rows, hc(ha)].astype(F32)
                k_b = k_ref[n, rows, hc(hb)].astype(F32)
                kt = _cat0(k_a, k_b).T
                kb_a = k_a * col(gb[n], GDN_HEADS + ha)
                kb_b = k_b * col(gb[n], GDN_HEADS + hb)
                lhs = _cat0(_cat1(kb_a, kb_b).astype(BF16),
                            _cat1(q_ref[n, rows, hc(ha)], q_ref[n, rows, hc(hb)]))
                kq = _dot(lhs, bdiag128(kt.astype(BF16)))
                a_mats[n, p] = jnp.where(strict, kq[:CHUNK] * decay, 0.0)
                attns[n, p] = (kq[CHUNK:] * decay).astype(BF16)
                kdts[n, p] = (kt * jnp.exp(g_last - gc_row)).astype(BF16)
                kbs[n, p] = (kb_a, kb_b)
                egs[n, p] = (jnp.exp(col(gcum[n], ha)), jnp.exp(col(gcum[n], hb)))

        cb, x = [], []
        for n in seqs:
            ad_pk = _dot(_cat1(*[jnp.where(bd16, a_mats[n, p], 0.0) for p in pairs]).astype(BF16), pack_ref[...])
            cb.append(_dot(ad_pk.astype(BF16), bcast_ref[...]))
            x.append([[eye_parts[0], eye_parts[1]] for _ in range(CHUNK // nb)])
        for j in range(nb - 1):
            tj, rj = divmod(j, SUBLANES)
            for n in seqs:
                cbj = cb[n][:, j * LANES:(j + 1) * LANES]
                for b in range(CHUNK // nb):
                    r = jnp.broadcast_to(x[n][b][tj][rj:rj + 1, :], (SUBLANES, LANES))
                    if j < SUBLANES - 1:
                        x[n][b][0] = x[n][b][0] - cbj[nb * b:nb * b + SUBLANES] * r
                    x[n][b][1] = x[n][b][1] - cbj[nb * b + SUBLANES:nb * (b + 1)] * r
        x_un = [_dot(_cat0(*[part for blk in x[n] for part in blk]).astype(BF16), unpack_ref[...]) for n in seqs]

        keys = [(n, p) for n in seqs for p in pairs]
        xd = {k: jnp.where(bd16, x_un[k[0]][:, k[1] * LANES:(k[1] + 1) * LANES], 0.0) for k in keys}
        m1 = {k: _dot(xd[k].astype(BF16), bdiag64(jnp.where(off32, a_mats[k], 0.0).astype(BF16))) for k in keys}
        x32 = {k: xd[k] - _dot(m1[k].astype(BF16), bdiag64(xd[k].astype(BF16))) for k in keys}
        m2 = {k: _dot(x32[k].astype(BF16), bdiag64(jnp.where(off64, a_mats[k], 0.0).astype(BF16))) for k in keys}
        tinv = {k: (x32[k] - _dot(m2[k].astype(BF16), bdiag64(x32[k].astype(BF16)))).astype(BF16) for k in keys}

        yws = {}
        for n, p in keys:
            ha, hb = 2 * p, 2 * p + 1
            rhs_a = _cat1(v_ref[n, rows, hc(ha)].astype(F32) * col(gb[n], GDN_HEADS + ha),
                          kbs[n, p][0] * egs[n, p][0]).astype(BF16)
            rhs_b = _cat1(v_ref[n, rows, hc(hb)].astype(F32) * col(gb[n], GDN_HEADS + hb),
                          kbs[n, p][1] * egs[n, p][1]).astype(BF16)
            z = jnp.zeros_like(rhs_a)
            yws[n, p] = _dot(tinv[n, p], _cat0(_cat1(rhs_a, z), _cat1(z, rhs_b)))

        wq = {}
        for n, p in keys:
            ha, hb = 2 * p, 2 * p + 1
            yw = yws[n, p]
            sb = state_ref[n, p].astype(BF16)
            z = jnp.zeros((GDN_DK, GDN_DK), BF16)
            s_bd = _cat0(_cat1(sb[:GDN_DK], z), _cat1(z, sb[GDN_DK:]))
            qd_a = q_ref[n, rows, hc(ha)].astype(F32) * egs[n, p][0]
            qd_b = q_ref[n, rows, hc(hb)].astype(F32) * egs[n, p][1]
            lhs = _cat0(_cat1(yw[:, 1 * GDN_DK:2 * GDN_DK], yw[:, 3 * GDN_DK:4 * GDN_DK]),
                        _cat1(qd_a, qd_b)).astype(BF16)
            wq[n, p] = _dot(lhs, s_bd)
        outs, upds = {}, {}
        for n, p in keys:
            yw = yws[n, p]
            v_new = _cat1(yw[:, 0:GDN_DK], yw[:, 2 * GDN_DK:3 * GDN_DK]) - wq[n, p][:CHUNK]
            vnb = v_new.astype(BF16)
            zc = jnp.zeros((CHUNK, GDN_DK), BF16)
            vn_bd = _cat0(_cat1(vnb[:, :GDN_DK], zc), _cat1(zc, vnb[:, GDN_DK:]))
            outs[n, p] = wq[n, p][CHUNK:] + _dot(attns[n, p], vn_bd)
            upds[n, p] = _dot(bdiag128(kdts[n, p]), _cat0(vnb[:, :GDN_DK], vnb[:, GDN_DK:]))
        for n, p in keys:
            ha, hb = 2 * p, 2 * p + 1
            e_a = jnp.exp(gcum[n][CHUNK - 1:CHUNK, ha:ha + 1])
            e_b = jnp.exp(gcum[n][CHUNK - 1:CHUNK, hb:hb + 1])
            state_ref[n, p, 0:GDN_DK, :] = state_ref[n, p, 0:GDN_DK, :] * e_a + upds[n, p][:GDN_DK]
            state_ref[n, p, GDN_DK:2 * GDN_DK, :] = (state_ref[n, p, GDN_DK:2 * GDN_DK, :] * e_b
                                                     + upds[n, p][GDN_DK:])
            for s_idx, h in enumerate((ha, hb)):
                o_h = outs[n, p][:, s_idx * GDN_DK:(s_idx + 1) * GDN_DK]
                on = _rms(o_h, on_ref[...]) * _silu(gate_ref[n, rows, hc(h)].astype(F32))
                o_ref[n, rows, hc(h)] = on.astype(BF16)
        return carry

    lax.fori_loop(0, tc // CHUNK, chunk, 0)


def _delta(q, k, v, gate, gb, out_norm, batch, seq, tc=256):
    m, d = q.shape
    ns = max(n for n in (4, 2, 1) if batch % n == 0)
    sel = _delta_selectors()
    row = lambda n: pl.BlockSpec((ns, tc, n), lambda b, t: (b, t, 0))
    as3d = lambda a: a.reshape(batch, seq, a.shape[-1])
    out = pl.pallas_call(
        functools.partial(_delta_kernel, tc=tc, ns=ns),
        grid=(batch // ns, seq // tc),
        in_specs=[row(d), row(d), row(d), row(d), row(LANES), _resident((1, GDN_DK))]
        + [_resident(s.shape) for s in sel],
        out_specs=row(d),
        out_shape=jax.ShapeDtypeStruct((batch, seq, d), BF16),
        scratch_shapes=[pltpu.VMEM((ns, GDN_PAIRS, 2 * GDN_DK, GDN_DK), F32)],
        compiler_params=_params(("parallel", "arbitrary")),
        name="gdn_delta",
    )(as3d(q), as3d(k), as3d(v), as3d(gate), as3d(gb), out_norm.reshape(1, GDN_DK), *sel)
    return out.reshape(m, d)


def _rope_angles(pos_ref, inv_ref):
    ang = pos_ref[...].astype(F32) * inv_ref[...]
    return jnp.cos(ang), jnp.sin(ang)


def _mla_kv_kernel(x_ref, g_ref, wc_ref, wr_ref, wrr_ref, g2_ref, wkb_ref, wvbt_ref,
                   pos_ref, inv_ref, k_ref, vt_ref):
    xn = _rms(x_ref[...], g_ref[...]).astype(BF16)
    cn = _rms(_dot(xn, wc_ref[...]), g2_ref[...]).astype(BF16)
    cos, sin = _rope_angles(pos_ref, inv_ref)
    kr = _dot(xn, wr_ref[...]) * cos + _dot(xn, wrr_ref[...]) * sin
    low = lax.broadcasted_iota(jnp.int32, kr.shape, 1) < MLA_ROPE
    kr_halves = (jnp.where(low, kr, 0.0).astype(BF16), jnp.where(low, 0.0, kr).astype(BF16))
    kn = _dot(cn, wkb_ref[...]).astype(BF16)
    for h in range(MLA_HEADS):
        k_ref[:, h * MLA_QK:h * MLA_QK + MLA_NOPE] = kn[:, h * MLA_NOPE:(h + 1) * MLA_NOPE]
        k_ref[:, h * MLA_QK + MLA_NOPE:(h + 1) * MLA_QK] = kr_halves[h % 2]
    vt = _dot_nt(wvbt_ref[...], cn).astype(BF16)
    ones = jnp.ones((MLA_VX - MLA_V, vt.shape[1]), BF16)
    for c in range(vt_ref.shape[0]):
        lc = slice(c * ATTN_TILE, (c + 1) * ATTN_TILE)
        for h in range(MLA_HEADS):
            vt_ref[c, h * MLA_VX:h * MLA_VX + MLA_V, :] = vt[h * MLA_V:(h + 1) * MLA_V, lc]
            vt_ref[c, h * MLA_VX + MLA_V:(h + 1) * MLA_VX, :] = ones[:, lc]


def _mla_kv(h, g, wc, wr, wrr, g2, wkb, wvbt, pos, inv, batch, seq, tm=512):
    m, d = h.shape
    n = wkb.shape[1]
    tps = seq // tm
    row = lambda w: pl.BlockSpec((tm, w), lambda i: (i, 0))
    return pl.pallas_call(
        _mla_kv_kernel,
        grid=(m // tm,),
        in_specs=[row(d), _resident((1, d)), _resident(wc.shape), _resident(wr.shape),
                  _resident(wrr.shape), _resident((1, wc.shape[1])), _resident(wkb.shape),
                  _resident(wvbt.shape), row(1), _resident((1, LANES))],
        out_specs=[row(MLA_HEADS * MLA_QK),
                   pl.BlockSpec((None, tm // ATTN_TILE, MLA_HEADS * MLA_VX, ATTN_TILE),
                                lambda i: (i // tps, i % tps, 0, 0))],
        out_shape=[jax.ShapeDtypeStruct((m, MLA_HEADS * MLA_QK), BF16),
                   jax.ShapeDtypeStruct((batch, seq // ATTN_TILE, MLA_HEADS * MLA_VX, ATTN_TILE), BF16)],
        compiler_params=_params(("parallel",)),
        name="mla_kv",
    )(h, g.reshape(1, d), wc, wr, wrr, g2.reshape(1, -1), wkb, wvbt, pos, inv)


def _mla_q_kernel(x_ref, g_ref, wdq_ref, g2_ref, wqnt_ref, wqrt_ref, wqrrt_ref, pos_ref, inv_ref,
                  qt_ref, *, scale):
    xn = _rms(x_ref[...], g_ref[...]).astype(BF16)
    ql = _rms(_dot(xn, wdq_ref[...]), g2_ref[...]).astype(BF16)
    qnt = (_dot_nt(wqnt_ref[...], ql) * scale).astype(BF16)
    half = MLA_ROPE // 2
    ang = inv_ref[0:half, :] * pos_ref[...].astype(F32)
    cos = _cat0(*[jnp.cos(ang) * scale] * (LANES // half))
    sin = _cat0(*[jnp.sin(ang) * scale] * (LANES // half))
    qrt = _dot_nt(wqrt_ref[...], ql)
    qrrt = _dot_nt(wqrrt_ref[...], ql)
    low = lax.broadcasted_iota(jnp.int32, cos.shape, 0) < MLA_ROPE
    for p in range(MLA_HEADS // 2):
        pr = slice(p * LANES, (p + 1) * LANES)
        r = qrt[pr] * cos + qrrt[pr] * sin
        halves = (jnp.where(low, r, 0.0).astype(BF16), jnp.where(low, 0.0, r).astype(BF16))
        for s in range(2):
            h = 2 * p + s
            for c in range(qt_ref.shape[0]):
                lc = slice(c * ATTN_TILE, (c + 1) * ATTN_TILE)
                qt_ref[c, h * MLA_QK:h * MLA_QK + MLA_NOPE, :] = qnt[h * MLA_NOPE:(h + 1) * MLA_NOPE, lc]
                qt_ref[c, h * MLA_QK + MLA_NOPE:(h + 1) * MLA_QK, :] = halves[s][:, lc]


def _mla_q(h, g, wdq, g2, wqnt, wqrt, wqrrt, pos_rows, inv_col, scale, batch, seq, tm=Q_TILE):
    m, d = h.shape
    tps = seq // tm
    per = tm // ATTN_TILE
    row = lambda w: pl.BlockSpec((tm, w), lambda i: (i, 0))
    return pl.pallas_call(
        functools.partial(_mla_q_kernel, scale=scale),
        grid=(m // tm,),
        in_specs=[row(d), _resident((1, d)), _resident(wdq.shape), _resident((1, wdq.shape[1])),
                  _resident(wqnt.shape), _resident(wqrt.shape), _resident(wqrrt.shape),
                  pl.BlockSpec((None, 1, tm), lambda i: (i, 0, 0)), _resident((LANES, 1))],
        out_specs=pl.BlockSpec((None, per, MLA_HEADS * MLA_QK, ATTN_TILE),
                               lambda i: (i // tps, i % tps, 0, 0)),
        out_shape=jax.ShapeDtypeStruct((batch, seq // ATTN_TILE, MLA_HEADS * MLA_QK, ATTN_TILE), BF16),
        compiler_params=_params(("parallel",)),
        name="mla_q",
    )(h, g.reshape(1, d), wdq, g2.reshape(1, -1), wqnt, wqrt, wqrrt, pos_rows, inv_col)


def _attn_kernel(qt_ref, k_ref, vt_ref, o_ref, m_ref, acc_ref, *, t):
    qi = pl.program_id(1)
    key_chunk = lax.broadcasted_iota(jnp.int32, (t, t), 0) // CHUNK
    qry_chunk = lax.broadcasted_iota(jnp.int32, (t, t), 1) // CHUNK
    diag_mask = qry_chunk >= key_chunk

    m_ref[...] = jnp.full(m_ref.shape, -jnp.inf, F32)
    acc_ref[...] = jnp.zeros(acc_ref.shape, F32)

    def scores(j, h):
        rows = pl.ds(pl.multiple_of(j * t, t), t)
        hq = slice(h * MLA_QK, (h + 1) * MLA_QK)
        return _dot(k_ref[rows, hq], qt_ref[hq, :])

    def step(j, pending, masked):
        rows = pl.ds(pl.multiple_of(j * t, t), t)
        pending = list(pending)
        for h in range(MLA_HEADS):
            hv = slice(h * MLA_VX, (h + 1) * MLA_VX)
            s = pending.pop(0)
            ahead = h + SCORE_LOOKAHEAD
            if ahead < MLA_HEADS:
                pending.append(scores(j, ahead))
            elif not masked:
                pending.append(scores(j + 1, ahead - MLA_HEADS))
            if masked:
                s = jnp.where(diag_mask, s, -jnp.inf)
            m_old = m_ref[h]
            m_new = jnp.maximum(m_old, jnp.max(s, axis=0, keepdims=True))
            alpha = jnp.exp2(m_old - m_new)
            p = jnp.exp2(s - m_new)
            acc_ref[h] = alpha * acc_ref[h] + _dot(vt_ref[j, hv, :], p.astype(BF16))
            m_ref[h] = m_new
        return tuple(pending)

    pending = tuple(scores(0, h) for h in range(SCORE_LOOKAHEAD))
    pending = lax.fori_loop(0, qi, lambda j, c: step(j, c, False), pending)
    step(qi, pending, True)
    for h in range(MLA_HEADS):
        acc = acc_ref[h]
        o_ref[:, h * MLA_V:(h + 1) * MLA_V] = (acc[:MLA_V] / acc[MLA_V:MLA_V + 1]).T.astype(BF16)


def _attn(qt, k, vt, batch, seq, t=ATTN_TILE):
    m, dq = k.shape
    dv = MLA_HEADS * MLA_V
    nq = seq // t
    return pl.pallas_call(
        functools.partial(_attn_kernel, t=t),
        grid=(batch, nq),
        in_specs=[pl.BlockSpec((None, None, dq, t), lambda b, i: (b, i, 0, 0)),
                  pl.BlockSpec((seq, dq), lambda b, i: (b, 0)),
                  pl.BlockSpec((None, nq, MLA_HEADS * MLA_VX, t), lambda b, i: (b, 0, 0, 0))],
        out_specs=pl.BlockSpec((t, dv), lambda b, i: (b * nq + i, 0)),
        out_shape=jax.ShapeDtypeStruct((m, dv), BF16),
        scratch_shapes=[pltpu.VMEM((MLA_HEADS, 1, t), F32), pltpu.VMEM((MLA_HEADS, MLA_VX, t), F32)],
        compiler_params=_params(("parallel", "arbitrary")),
        name="mla_attn",
    )(qt, k, vt)


def _rot_half_cols(w, heads):
    k = w.shape[0]
    w = w.reshape(k, heads, 2, MLA_ROPE // 2)
    return jnp.stack([-w[:, :, 1], w[:, :, 0]], axis=2).reshape(k, heads * MLA_ROPE)


def kernel(x, positions, ffn1_norm, ffn1_w_gu, ffn1_w_down, mix_norm, ffn2_norm, ffn2_w_gu, ffn2_w_down, gdn_w_in, gdn_conv_w, gdn_a_log, gdn_dt_bias, gdn_out_norm, gdn_w_out, kv_norm, mla_w_kv_a, mla_kv_a_norm, mla_w_kv_b, mla_w_dq, mla_q_norm, mla_w_uq, mla_w_o, final_norm):
    batch, seq, d = x.shape
    m = batch * seq
    depth = ffn1_norm.shape[0]
    n_a = gdn_w_in.shape[0]
    d_qk = GDN_HEADS * GDN_DK
    bf = lambda w: w.astype(BF16)

    h = x.reshape(m, d)
    pos = positions.reshape(m, 1)
    pos_rows = positions.reshape(m // Q_TILE, 1, Q_TILE)
    half = MLA_ROPE // 2
    inv = ROPE_THETA ** (-jnp.arange(half, dtype=F32) / half)
    inv = jnp.tile(inv, LANES // half).reshape(1, LANES)
    scale = (MLA_NOPE + MLA_ROPE) ** -0.5 * LOG2_E

    w1_gu, w1_down, w2_gu, w2_down = bf(ffn1_w_gu), bf(ffn1_w_down), bf(ffn2_w_gu), bf(ffn2_w_down)
    kk = vt = None
    for layer in range(depth):
        h = _ffn(h, ffn1_norm[layer], w1_gu, w1_down, layer)
        if layer < n_a:
            i = layer
            w_in = gdn_w_in[i]
            wab = jnp.pad(w_in[:, 4 * d_qk:], ((0, 0), (0, LANES - 2 * GDN_HEADS)))
            pad_row = lambda p: jnp.pad(p.astype(F32), (0, LANES - GDN_HEADS)).reshape(1, LANES)
            q, k, v, gate, gb = _gdn_in(
                h, mix_norm[layer], bf(w_in[:, :3 * d_qk]), bf(w_in[:, 3 * d_qk:4 * d_qk]), bf(wab),
                gdn_conv_w[i], pad_row(gdn_a_log[i]), pad_row(gdn_dt_bias[i]), seq)
            mixed = _delta(q, k, v, gate, gb, gdn_out_norm[i], batch, seq)
            w_proj = bf(gdn_w_out[i])
        else:
            j = layer - n_a
            w_uq = mla_w_uq[j].reshape(-1, MLA_HEADS, MLA_NOPE + MLA_ROPE)
            wqn = w_uq[:, :, :MLA_NOPE].reshape(-1, MLA_HEADS * MLA_NOPE)
            wqr = w_uq[:, :, MLA_NOPE:].reshape(-1, MLA_HEADS * MLA_ROPE)
            qt = _mla_q(h, mix_norm[layer], bf(mla_w_dq[j]), mla_q_norm[j], bf(wqn.T), bf(wqr.T),
                        bf(_rot_half_cols(wqr, MLA_HEADS).T), pos_rows, inv.reshape(LANES, 1), scale,
                        batch, seq)
            mixed = _attn(qt, kk, vt, batch, seq)
            w_proj = bf(mla_w_o[j])
        last = layer == depth - 1
        h = _ffn(h, ffn2_norm[layer], w2_gu, w2_down, layer,
                 proj=(mixed, w_proj), final_g=final_norm if last else None)
        if layer == n_a - 1:
            wr = mla_w_kv_a[:, MLA_KV_RANK:]
            w_kv_b = mla_w_kv_b.reshape(MLA_KV_RANK, MLA_HEADS, MLA_NOPE + MLA_V)
            twice = lambda w: jnp.concatenate([w, w], axis=1)
            kk, vt = _mla_kv(
                h, kv_norm, bf(mla_w_kv_a[:, :MLA_KV_RANK]), bf(twice(wr)),
                bf(twice(_rot_half_cols(wr, 1))), mla_kv_a_norm,
                bf(w_kv_b[:, :, :MLA_NOPE].reshape(MLA_KV_RANK, -1)),
                bf(w_kv_b[:, :, MLA_NOPE:].reshape(MLA_KV_RANK, -1).T), pos, inv, batch, seq)
    return h.reshape(batch, seq, d)
```

```python
import functools

import jax
import jax.numpy as jnp
import numpy as np
from jax import lax
from jax.experimental import pallas as pl
from jax.experimental.pallas import tpu as pltpu

F32 = jnp.float32
BF16 = jnp.bfloat16

NORM_EPS = 1e-6
CHUNK = 64
GDN_HEADS = 8
GDN_PAIRS = GDN_HEADS // 2
GDN_DK = 128
GDN_CONV = 4
MLA_HEADS = 8
MLA_NOPE = 128
MLA_ROPE = 64
MLA_V = 128
MLA_VX = MLA_V + 16
MLA_KV_RANK = 256
MLA_QK = 256
LOG2_E = 1.4426950408889634
ROPE_THETA = 10000.0
LANES = 128
SUBLANES = 8
INV_BLOCK = 16
SCORE_LOOKAHEAD = 3
Q_TILE = 512
ATTN_TILE = 256
FFN_TILE = 1024
CONV_ROWS = 128
PROJ_COLS = 256
VMEM_LIMIT = 56 * 1024 * 1024


def _resident(shape):
    nd = len(shape)
    return pl.BlockSpec(shape, lambda *_: (0,) * nd, pipeline_mode=pl.Buffered(1))


def _resident_layer(stacked_shape, layer):
    nd = len(stacked_shape) - 1
    return pl.BlockSpec((None,) + tuple(stacked_shape[1:]), lambda *_: (layer,) + (0,) * nd,
                        pipeline_mode=pl.Buffered(1))


def _params(sem):
    return pltpu.CompilerParams(dimension_semantics=sem, vmem_limit_bytes=VMEM_LIMIT)


def _rms(x, g):
    ms = jnp.mean(x * x, axis=-1, keepdims=True)
    return x * lax.rsqrt(ms + NORM_EPS) * g


def _silu(x):
    return x * jax.nn.sigmoid(x)


def _dot(a, b):
    return jnp.dot(a, b, preferred_element_type=F32)


def _dot_nt(a, b):
    return lax.dot_general(a, b, (((1,), (1,)), ((), ())), preferred_element_type=F32)


def _cat0(*xs):
    return jnp.concatenate(xs, axis=0)


def _cat1(*xs):
    return jnp.concatenate(xs, axis=1)


def _ffn_kernel(*refs, d_ff, tf, proj, final):
    refs = list(refs)
    x_ref = refs.pop(0)
    if proj:
        a_ref, wp_ref = refs.pop(0), refs.pop(0)
    g_ref, wgu_ref, wd_ref = refs.pop(0), refs.pop(0), refs.pop(0)
    if final:
        gf_ref = refs.pop(0)
    o_ref, xn_ref, act_ref = refs
    if proj:
        o_ref[...] = x_ref[...] + _dot(a_ref[...], wp_ref[...])
        h_ref = o_ref
    else:
        h_ref = x_ref
    xn_ref[...] = _rms(h_ref[...], g_ref[...]).astype(BF16)
    for c in range(d_ff // tf):
        g = _dot(xn_ref[...], wgu_ref[:, c * tf:(c + 1) * tf])
        u = _dot(xn_ref[...], wgu_ref[:, d_ff + c * tf:d_ff + (c + 1) * tf])
        act_ref[:, c * tf:(c + 1) * tf] = (_silu(g) * u).astype(BF16)
    y = h_ref[...] + 0.5 * _dot(act_ref[...], wd_ref[...])
    if final:
        y = _rms(y, gf_ref[...])
    o_ref[...] = y


def _ffn(h, g, wgu, wd, layer, proj=None, final_g=None, tm=FFN_TILE, tf=256):
    m, d = h.shape
    d_ff = wd.shape[1]
    row = lambda n: pl.BlockSpec((tm, n), lambda i: (i, 0))
    in_specs, args = [row(d)], [h]
    if proj is not None:
        a, wp = proj
        in_specs += [row(a.shape[1]), _resident(wp.shape)]
        args += [a, wp]
    in_specs += [_resident((1, d)), _resident_layer(wgu.shape, layer), _resident_layer(wd.shape, layer)]
    args += [g.reshape(1, d), wgu, wd]
    if final_g is not None:
        in_specs.append(_resident((1, d)))
        args.append(final_g.reshape(1, d))
    return pl.pallas_call(
        functools.partial(_ffn_kernel, d_ff=d_ff, tf=tf, proj=proj is not None,
                          final=final_g is not None),
        grid=(m // tm,),
        in_specs=in_specs,
        out_specs=row(d),
        out_shape=jax.ShapeDtypeStruct((m, d), F32),
        scratch_shapes=[pltpu.VMEM((tm, d), BF16), pltpu.VMEM((tm, d_ff), BF16)],
        compiler_params=_params(("parallel",)),
        name="ffn" + ("_proj" if proj is not None else "") + ("_final" if final_g is not None else ""),
    )(*args)


def _gdn_in_kernel(x_ref, g_ref, wqkv_ref, wgate_ref, wab_ref, cw_ref, alog_ref, dtb_ref,
                   q_ref, k_ref, v_ref, gate_ref, gb_ref, xn_ref, pbuf_ref,
                   *, tm, tiles_per_seq, d_qk):
    i = pl.program_id(0)
    xn_ref[...] = _rms(x_ref[...], g_ref[...]).astype(BF16)

    @pl.when(i % tiles_per_seq == 0)
    def _():
        pbuf_ref[0:SUBLANES, :] = jnp.zeros((SUBLANES, pbuf_ref.shape[1]), F32)

    outs = (q_ref, k_ref, v_ref)
    heads_per_piece = PROJ_COLS // GDN_DK
    pieces = [(part, g) for part in range(3) for g in range(d_qk // PROJ_COLS)]

    def project(part, g):
        cols = slice(part * d_qk + g * PROJ_COLS, part * d_qk + (g + 1) * PROJ_COLS)
        pbuf_ref[SUBLANES:SUBLANES + tm, cols] = _dot(xn_ref[...], wqkv_ref[:, cols])

    def conv_act_norm(part, g):
        scale = GDN_DK ** -0.5 if part == 0 else 1.0
        for h in range(g * heads_per_piece, (g + 1) * heads_per_piece):
            hc = slice(h * GDN_DK, (h + 1) * GDN_DK)
            cols = slice(part * d_qk + h * GDN_DK, part * d_qk + (h + 1) * GDN_DK)
            for r0 in range(0, tm, CONV_ROWS):
                y = cw_ref[GDN_CONV - 1:GDN_CONV, cols] * pbuf_ref[SUBLANES + r0:SUBLANES + r0 + CONV_ROWS, cols]
                for j in range(GDN_CONV - 1):
                    off = SUBLANES - (GDN_CONV - 1) + j + r0
                    y = y + cw_ref[j:j + 1, cols] * pbuf_ref[off:off + CONV_ROWS, cols]
                y = _silu(y)
                if part < 2:
                    ss = jnp.sum(y * y, axis=-1, keepdims=True)
                    y = y * (lax.rsqrt(ss + NORM_EPS) * scale)
                outs[part][r0:r0 + CONV_ROWS, hc] = y.astype(BF16)

    def gate(g):
        cols = slice(g * PROJ_COLS, (g + 1) * PROJ_COLS)
        gate_ref[:, cols] = _dot(xn_ref[...], wgate_ref[:, cols]).astype(gate_ref.dtype)

    n_gate = d_qk // PROJ_COLS
    project(*pieces[0])
    for idx, piece in enumerate(pieces):
        if idx + 1 < len(pieces):
            project(*pieces[idx + 1])
        elif n_gate:
            gate(0)
        conv_act_norm(*piece)
        tail = idx - (len(pieces) - n_gate)
        if 0 <= tail < n_gate - 1:
            gate(tail + 1)
    pbuf_ref[0:SUBLANES, :] = pbuf_ref[tm:tm + SUBLANES, :]

    ab = _dot(xn_ref[...], wab_ref[...])
    x = ab + dtb_ref[...]
    softplus = jnp.maximum(x, 0.0) + jnp.log1p(jnp.exp(-jnp.abs(x)))
    decay = -jnp.exp(alog_ref[...]) * softplus
    lane = lax.broadcasted_iota(jnp.int32, ab.shape, 1)
    gb_ref[...] = jnp.where(lane < GDN_HEADS, decay, jax.nn.sigmoid(ab))


def _gdn_in(h, g, wqkv, wgate, wab, conv_w, alog_row, dtb_row, seq, tm=512):
    m, d = h.shape
    d_qk = wgate.shape[1]
    row = lambda n: pl.BlockSpec((tm, n), lambda i: (i, 0))
    return pl.pallas_call(
        functools.partial(_gdn_in_kernel, tm=tm, tiles_per_seq=seq // tm, d_qk=d_qk),
        grid=(m // tm,),
        in_specs=[row(d), _resident((1, d)), _resident(wqkv.shape), _resident(wgate.shape),
                  _resident(wab.shape), _resident(conv_w.shape), _resident((1, LANES)),
                  _resident((1, LANES))],
        out_specs=[row(d_qk), row(d_qk), row(d_qk), row(d_qk), row(LANES)],
        out_shape=[jax.ShapeDtypeStruct((m, d_qk), BF16)] * 4 + [jax.ShapeDtypeStruct((m, LANES), F32)],
        scratch_shapes=[pltpu.VMEM((tm, d), BF16), pltpu.VMEM((tm + SUBLANES, 3 * d_qk), F32)],
        compiler_params=_params(("arbitrary",)),
        name="gdn_in",
    )(h, g.reshape(1, d), wqkv, wgate, wab, conv_w, alog_row, dtb_row)


def _split3(x):
    hi = x.astype(BF16)
    r = x - hi.astype(F32)
    mid = r.astype(BF16)
    lo = (r - mid.astype(F32)).astype(BF16)
    return hi, mid, lo


def _delta_selectors():
    nb = INV_BLOCK
    pack = np.zeros((GDN_PAIRS * LANES, LANES), np.float32)
    for r in range(GDN_PAIRS * LANES):
        p, l = divmod(r, LANES)
        h, c = 2 * p + l // CHUNK, l % CHUNK
        pack[r, nb * h + c % nb] = 1.0
    bcast = np.zeros((LANES, (nb - 1) * LANES), np.float32)
    for j in range(nb - 1):
        for c in range(LANES):
            bcast[nb * (c // nb) + j, LANES * j + c] = 1.0
    unpack = np.zeros((LANES, GDN_HEADS * CHUNK), np.float32)
    for h in range(GDN_HEADS):
        for c in range(CHUNK):
            unpack[nb * h + c % nb, CHUNK * h + c] = 1.0
    return tuple(jnp.asarray(a, BF16) for a in (pack, bcast, unpack))


def _delta_kernel(q_ref, k_ref, v_ref, gate_ref, gb_ref, on_ref, pack_ref, bcast_ref, unpack_ref,
                  o_ref, state_ref, *, tc, ns):
    @pl.when(pl.program_id(1) == 0)
    def _():
        state_ref[...] = jnp.zeros(state_ref.shape, F32)

    nb = INV_BLOCK
    ri = lax.broadcasted_iota(jnp.int32, (CHUNK, LANES), 0)
    li = lax.broadcasted_iota(jnp.int32, (CHUNK, LANES), 1)
    ci = li % CHUNK
    lo64 = li < CHUNK
    lo64_row = lo64[0:1]
    causal = ri >= ci
    strict = ri > ci
    bd16 = (ri // nb) == (ci // nb)
    bd32 = (ri // (2 * nb)) == (ci // (2 * nb))
    off32 = jnp.logical_and(bd32, jnp.logical_not(bd16))
    off64 = jnp.logical_not(bd32)
    lo128 = lax.broadcasted_iota(jnp.int32, (LANES, LANES), 1) < CHUNK
    tril = jnp.where(lax.broadcasted_iota(jnp.int32, (CHUNK, CHUNK), 0)
                     >= lax.broadcasted_iota(jnp.int32, (CHUNK, CHUNK), 1), 1.0, 0.0).astype(BF16)
    sub_i = lax.broadcasted_iota(jnp.int32, (SUBLANES, LANES), 0)
    sub_l = lax.broadcasted_iota(jnp.int32, (SUBLANES, LANES), 1) % nb
    eye_parts = [jnp.where(sub_l == sub_i + SUBLANES * t, 1.0, 0.0).astype(F32) for t in range(2)]

    def bdiag64(x):
        z = jnp.zeros_like(x)
        return _cat0(jnp.where(lo64, x, z), jnp.where(lo64, z, x))

    def bdiag128(x):
        z = jnp.zeros_like(x)
        return _cat0(jnp.where(lo128, x, z), jnp.where(lo128, z, x))

    hc = lambda h: slice(h * GDN_DK, (h + 1) * GDN_DK)
    col = lambda x, h: x[:, h:h + 1]
    seqs = range(ns)
    pairs = range(GDN_PAIRS)

    def chunk(c, carry):
        rows = pl.ds(pl.multiple_of(c * CHUNK, CHUNK), CHUNK)

        gb, gcum, gcum_t = [], [], []
        for n in seqs:
            gb.append(gb_ref[n, rows, :])
            hi, mid, lo = _split3(gb[n])
            gcum.append(_dot(tril, hi) + _dot(tril, mid) + _dot(tril, lo))
            gcum_t.append(_cat0(gcum[n], gcum[n]).T)

        a_mats, attns, kdts, kbs, egs = {}, {}, {}, {}, {}
        for n in seqs:
            for p in pairs:
                ha, hb = 2 * p, 2 * p + 1
                gc_col = jnp.where(lo64, col(gcum[n], ha), col(gcum[n], hb))
                gc_row = jnp.where(lo64_row, gcum_t[n][ha:ha + 1, :], gcum_t[n][hb:hb + 1, :])
                g_last = jnp.where(lo64_row, gcum[n][CHUNK - 1:CHUNK, ha:ha + 1],
                                   gcum[n][CHUNK - 1:CHUNK, hb:hb + 1])
                decay = jnp.where(causal, jnp.exp(jnp.where(causal, gc_col - gc_row, 0.0)), 0.0)
                k_a = k_ref[n, rows, hc(ha)].astype(F32)
                k_b = k_ref[n, rows, hc(hb)].astype(F32)
                kt = _cat0(k_a, k_b).T
                kb_a = k_a * col(gb[n], GDN_HEADS + ha)
                kb_b = k_b * col(gb[n], GDN_HEADS + hb)
                lhs = _cat0(_cat1(kb_a, kb_b).astype(BF16),
                            _cat1(q_ref[n, rows, hc(ha)], q_ref[n, rows, hc(hb)]))
                kq = _dot(lhs, bdiag128(kt.astype(BF16)))
                a_mats[n, p] = jnp.where(strict, kq[:CHUNK] * decay, 0.0)
                attns[n, p] = (kq[CHUNK:] * decay).astype(BF16)
                kdts[n, p] = (kt * jnp.exp(g_last - gc_row)).astype(BF16)
                kbs[n, p] = (kb_a, kb_b)
                egs[n, p] = (jnp.exp(col(gcum[n], ha)), jnp.exp(col(gcum[n], hb)))

        cb, x = [], []
        for n in seqs:
            ad_pk = _dot(_cat1(*[jnp.where(bd16, a_mats[n, p], 0.0) for p in pairs]).astype(BF16), pack_ref[...])
            cb.append(_dot(ad_pk.astype(BF16), bcast_ref[...]))
            x.append([[eye_parts[0], eye_parts[1]] for _ in range(CHUNK // nb)])
        for j in range(nb - 1):
            tj, rj = divmod(j, SUBLANES)
            for n in seqs:
                cbj = cb[n][:, j * LANES:(j + 1) * LANES]
                for b in range(CHUNK // nb):
                    r = jnp.broadcast_to(x[n][b][tj][rj:rj + 1, :], (SUBLANES, LANES))
                    if j < SUBLANES - 1:
                        x[n][b][0] = x[n][b][0] - cbj[nb * b:nb * b + SUBLANES] * r
                    x[n][b][1] = x[n][b][1] - cbj[nb * b + SUBLANES:nb * (b + 1)] * r
        x_un = [_dot(_cat0(*[part for blk in x[n] for part in blk]).astype(BF16), unpack_ref[...]) for n in seqs]

        keys = [(n, p) for n in seqs for p in pairs]
        xd = {k: jnp.where(bd16, x_un[k[0]][:, k[1] * LANES:(k[1] + 1) * LANES], 0.0) for k in keys}
        m1 = {k: _dot(xd[k].astype(BF16), bdiag64(jnp.where(off32, a_mats[k], 0.0).astype(BF16))) for k in keys}
        x32 = {k: xd[k] - _dot(m1[k].astype(BF16), bdiag64(xd[k].astype(BF16))) for k in keys}
        m2 = {k: _dot(x32[k].astype(BF16), bdiag64(jnp.where(off64, a_mats[k], 0.0).astype(BF16))) for k in keys}
        tinv = {k: (x32[k] - _dot(m2[k].astype(BF16), bdiag64(x32[k].astype(BF16)))).astype(BF16) for k in keys}

        yws = {}
        for n, p in keys:
            ha, hb = 2 * p, 2 * p + 1
            rhs_a = _cat1(v_ref[n, rows, hc(ha)].astype(F32) * col(gb[n], GDN_HEADS + ha),
                          kbs[n, p][0] * egs[n, p][0]).astype(BF16)
            rhs_b = _cat1(v_ref[n, rows, hc(hb)].astype(F32) * col(gb[n], GDN_HEADS + hb),
                          kbs[n, p][1] * egs[n, p][1]).astype(BF16)
            z = jnp.zeros_like(rhs_a)
            yws[n, p] = _dot(tinv[n, p], _cat0(_cat1(rhs_a, z), _cat1(z, rhs_b)))

        wq = {}
        for n, p in keys:
            ha, hb = 2 * p, 2 * p + 1
            yw = yws[n, p]
            sb = state_ref[n, p].astype(BF16)
            z = jnp.zeros((GDN_DK, GDN_DK), BF16)
            s_bd = _cat0(_cat1(sb[:GDN_DK], z), _cat1(z, sb[GDN_DK:]))
            qd_a = q_ref[n, rows, hc(ha)].astype(F32) * egs[n, p][0]
            qd_b = q_ref[n, rows, hc(hb)].astype(F32) * egs[n, p][1]
            lhs = _cat0(_cat1(yw[:, 1 * GDN_DK:2 * GDN_DK], yw[:, 3 * GDN_DK:4 * GDN_DK]),
                        _cat1(qd_a, qd_b)).astype(BF16)
            wq[n, p] = _dot(lhs, s_bd)
        outs, upds = {}, {}
        for n, p in keys:
            yw = yws[n, p]
            v_new = _cat1(yw[:, 0:GDN_DK], yw[:, 2 * GDN_DK:3 * GDN_DK]) - wq[n, p][:CHUNK]
            vnb = v_new.astype(BF16)
            zc = jnp.zeros((CHUNK, GDN_DK), BF16)
            vn_bd = _cat0(_cat1(vnb[:, :GDN_DK], zc), _cat1(zc, vnb[:, GDN_DK:]))
            outs[n, p] = wq[n, p][CHUNK:] + _dot(attns[n, p], vn_bd)
            upds[n, p] = _dot(bdiag128(kdts[n, p]), _cat0(vnb[:, :GDN_DK], vnb[:, GDN_DK:]))
        for n, p in keys:
            ha, hb = 2 * p, 2 * p + 1
            e_a = jnp.exp(gcum[n][CHUNK - 1:CHUNK, ha:ha + 1])
            e_b = jnp.exp(gcum[n][CHUNK - 1:CHUNK, hb:hb + 1])
            state_ref[n, p, 0:GDN_DK, :] = state_ref[n, p, 0:GDN_DK, :] * e_a + upds[n, p][:GDN_DK]
            state_ref[n, p, GDN_DK:2 * GDN_DK, :] = (state_ref[n, p, GDN_DK:2 * GDN_DK, :] * e_b
                                                     + upds[n, p][GDN_DK:])
            for s_idx, h in enumerate((ha, hb)):
                o_h = outs[n, p][:, s_idx * GDN_DK:(s_idx + 1) * GDN_DK]
                on = _rms(o_h, on_ref[...]) * _silu(gate_ref[n, rows, hc(h)].astype(F32))
                o_ref[n, rows, hc(h)] = on.astype(BF16)
        return carry

    lax.fori_loop(0, tc // CHUNK, chunk, 0)


def _delta(q, k, v, gate, gb, out_norm, batch, seq, tc=256):
    m, d = q.shape
    ns = max(n for n in (4, 2, 1) if batch % n == 0)
    sel = _delta_selectors()
    row = lambda n: pl.BlockSpec((ns, tc, n), lambda b, t: (b, t, 0))
    as3d = lambda a: a.reshape(batch, seq, a.shape[-1])
    out = pl.pallas_call(
        functools.partial(_delta_kernel, tc=tc, ns=ns),
        grid=(batch // ns, seq // tc),
        in_specs=[row(d), row(d), row(d), row(d), row(LANES), _resident((1, GDN_DK))]
        + [_resident(s.shape) for s in sel],
        out_specs=row(d),
        out_shape=jax.ShapeDtypeStruct((batch, seq, d), BF16),
        scratch_shapes=[pltpu.VMEM((ns, GDN_PAIRS, 2 * GDN_DK, GDN_DK), F32)],
        compiler_params=_params(("parallel", "arbitrary")),
        name="gdn_delta",
    )(as3d(q), as3d(k), as3d(v), as3d(gate), as3d(gb), out_norm.reshape(1, GDN_DK), *sel)
    return out.reshape(m, d)


def _rope_cos_sin_t(pos_ref, inv_ref, scale=1.0):
    half = MLA_ROPE // 2
    ang = inv_ref[0:half, :] * pos_ref[...].astype(F32)
    tile = lambda x: _cat0(*[x * scale] * (LANES // half))
    return tile(jnp.cos(ang)), tile(jnp.sin(ang))


def _mla_kv_kernel(x_ref, g_ref, wc_ref, wrt_ref, wrrt_ref, g2_ref, wkb_ref, wvbt_ref,
                   pos_ref, inv_ref, k_ref, vt_ref):
    xn = _rms(x_ref[...], g_ref[...]).astype(BF16)
    cn = _rms(_dot(xn, wc_ref[...]), g2_ref[...]).astype(BF16)
    cos, sin = _rope_cos_sin_t(pos_ref, inv_ref)
    kr_t = _dot_nt(wrt_ref[...], xn) * cos + _dot_nt(wrrt_ref[...], xn) * sin
    kr = kr_t.T
    low = lax.broadcasted_iota(jnp.int32, kr.shape, 1) < MLA_ROPE
    kr_halves = (jnp.where(low, kr, 0.0).astype(BF16), jnp.where(low, 0.0, kr).astype(BF16))
    kn = _dot(cn, wkb_ref[...]).astype(BF16)
    for h in range(MLA_HEADS):
        k_ref[:, h * MLA_QK:h * MLA_QK + MLA_NOPE] = kn[:, h * MLA_NOPE:(h + 1) * MLA_NOPE]
        k_ref[:, h * MLA_QK + MLA_NOPE:(h + 1) * MLA_QK] = kr_halves[h % 2]
    vt = _dot_nt(wvbt_ref[...], cn).astype(BF16)
    ones = jnp.ones((MLA_VX - MLA_V, vt.shape[1]), BF16)
    for c in range(vt_ref.shape[0]):
        lc = slice(c * ATTN_TILE, (c + 1) * ATTN_TILE)
        for h in range(MLA_HEADS):
            vt_ref[c, h * MLA_VX:h * MLA_VX + MLA_V, :] = vt[h * MLA_V:(h + 1) * MLA_V, lc]
            vt_ref[c, h * MLA_VX + MLA_V:(h + 1) * MLA_VX, :] = ones[:, lc]


def _mla_kv(h, g, wc, wrt, wrrt, g2, wkb, wvbt, pos_rows, inv_col, batch, seq, tm=Q_TILE):
    m, d = h.shape
    n = wkb.shape[1]
    tps = seq // tm
    row = lambda w: pl.BlockSpec((tm, w), lambda i: (i, 0))
    return pl.pallas_call(
        _mla_kv_kernel,
        grid=(m // tm,),
        in_specs=[row(d), _resident((1, d)), _resident(wc.shape), _resident(wrt.shape),
                  _resident(wrrt.shape), _resident((1, wc.shape[1])), _resident(wkb.shape),
                  _resident(wvbt.shape), pl.BlockSpec((None, 1, tm), lambda i: (i, 0, 0)),
                  _resident((LANES, 1))],
        out_specs=[row(MLA_HEADS * MLA_QK),
                   pl.BlockSpec((None, tm // ATTN_TILE, MLA_HEADS * MLA_VX, ATTN_TILE),
                                lambda i: (i // tps, i % tps, 0, 0))],
        out_shape=[jax.ShapeDtypeStruct((m, MLA_HEADS * MLA_QK), BF16),
                   jax.ShapeDtypeStruct((batch, seq // ATTN_TILE, MLA_HEADS * MLA_VX, ATTN_TILE), BF16)],
        compiler_params=_params(("parallel",)),
        name="mla_kv",
    )(h, g.reshape(1, d), wc, wrt, wrrt, g2.reshape(1, -1), wkb, wvbt, pos_rows, inv_col)


def _mla_q_kernel(x_ref, g_ref, wdq_ref, g2_ref, wqnt_ref, wqrt_ref, wqrrt_ref, pos_ref, inv_ref,
                  qt_ref, *, scale):
    xn = _rms(x_ref[...], g_ref[...]).astype(BF16)
    ql = _rms(_dot(xn, wdq_ref[...]), g2_ref[...]).astype(BF16)
    qnt = (_dot_nt(wqnt_ref[...], ql) * scale).astype(BF16)
    cos, sin = _rope_cos_sin_t(pos_ref, inv_ref, scale)
    qrt = _dot_nt(wqrt_ref[...], ql)
    qrrt = _dot_nt(wqrrt_ref[...], ql)
    low = lax.broadcasted_iota(jnp.int32, cos.shape, 0) < MLA_ROPE
    for p in range(MLA_HEADS // 2):
        pr = slice(p * LANES, (p + 1) * LANES)
        r = qrt[pr] * cos + qrrt[pr] * sin
        halves = (jnp.where(low, r, 0.0).astype(BF16), jnp.where(low, 0.0, r).astype(BF16))
        for s in range(2):
            h = 2 * p + s
            for c in range(qt_ref.shape[0]):
                lc = slice(c * ATTN_TILE, (c + 1) * ATTN_TILE)
                qt_ref[c, h * MLA_QK:h * MLA_QK + MLA_NOPE, :] = qnt[h * MLA_NOPE:(h + 1) * MLA_NOPE, lc]
                qt_ref[c, h * MLA_QK + MLA_NOPE:(h + 1) * MLA_QK, :] = halves[s][:, lc]


def _mla_q(h, g, wdq, g2, wqnt, wqrt, wqrrt, pos_rows, inv_col, scale, batch, seq, tm=Q_TILE):
    m, d = h.shape
    tps = seq // tm
    per = tm // ATTN_TILE
    row = lambda w: pl.BlockSpec((tm, w), lambda i: (i, 0))
    return pl.pallas_call(
        functools.partial(_mla_q_kernel, scale=scale),
        grid=(m // tm,),
        in_specs=[row(d), _resident((1, d)), _resident(wdq.shape), _resident((1, wdq.shape[1])),
                  _resident(wqnt.shape), _resident(wqrt.shape), _resident(wqrrt.shape),
                  pl.BlockSpec((None, 1, tm), lambda i: (i, 0, 0)), _resident((LANES, 1))],
        out_specs=pl.BlockSpec((None, per, MLA_HEADS * MLA_QK, ATTN_TILE),
                               lambda i: (i // tps, i % tps, 0, 0)),
        out_shape=jax.ShapeDtypeStruct((batch, seq // ATTN_TILE, MLA_HEADS * MLA_QK, ATTN_TILE), BF16),
        compiler_params=_params(("parallel",)),
        name="mla_q",
    )(h, g.reshape(1, d), wdq, g2.reshape(1, -1), wqnt, wqrt, wqrrt, pos_rows, inv_col)


def _attn_kernel(qt_ref, k_ref, vt_ref, o_ref, m_ref, acc_ref, *, t):
    qi = pl.program_id(1)
    key_chunk = lax.broadcasted_iota(jnp.int32, (t, t), 0) // CHUNK
    qry_chunk = lax.broadcasted_iota(jnp.int32, (t, t), 1) // CHUNK
    diag_mask = qry_chunk >= key_chunk

    m_ref[...] = jnp.full(m_ref.shape, -jnp.inf, F32)
    acc_ref[...] = jnp.zeros(acc_ref.shape, F32)

    def scores(j, h):
        rows = pl.ds(pl.multiple_of(j * t, t), t)
        hq = slice(h * MLA_QK, (h + 1) * MLA_QK)
        return _dot(k_ref[rows, hq], qt_ref[hq, :])

    def step(j, pending, masked):
        rows = pl.ds(pl.multiple_of(j * t, t), t)
        pending = list(pending)
        for h in range(MLA_HEADS):
            hv = slice(h * MLA_VX, (h + 1) * MLA_VX)
            s = pending.pop(0)
            ahead = h + SCORE_LOOKAHEAD
            if ahead < MLA_HEADS:
                pending.append(scores(j, ahead))
            elif not masked:
                pending.append(scores(j + 1, ahead - MLA_HEADS))
            if masked:
                s = jnp.where(diag_mask, s, -jnp.inf)
            m_old = m_ref[h]
            m_new = jnp.maximum(m_old, jnp.max(s, axis=0, keepdims=True))
            alpha = jnp.exp2(m_old - m_new)
            p = jnp.exp2(s - m_new)
            acc_ref[h] = alpha * acc_ref[h] + _dot(vt_ref[j, hv, :], p.astype(BF16))
            m_ref[h] = m_new
        return tuple(pending)

    pending = tuple(scores(0, h) for h in range(SCORE_LOOKAHEAD))
    pending = lax.fori_loop(0, qi, lambda j, c: step(j, c, False), pending)
    step(qi, pending, True)
    for h in range(MLA_HEADS):
        acc = acc_ref[h]
        o_ref[:, h * MLA_V:(h + 1) * MLA_V] = (acc[:MLA_V] / acc[MLA_V:MLA_V + 1]).T.astype(BF16)


def _attn(qt, k, vt, batch, seq, t=ATTN_TILE):
    m, dq = k.shape
    dv = MLA_HEADS * MLA_V
    nq = seq // t
    return pl.pallas_call(
        functools.partial(_attn_kernel, t=t),
        grid=(batch, nq),
        in_specs=[pl.BlockSpec((None, None, dq, t), lambda b, i: (b, i, 0, 0)),
                  pl.BlockSpec((seq, dq), lambda b, i: (b, 0)),
                  pl.BlockSpec((None, nq, MLA_HEADS * MLA_VX, t), lambda b, i: (b, 0, 0, 0))],
        out_specs=pl.BlockSpec((t, dv), lambda b, i: (b * nq + i, 0)),
        out_shape=jax.ShapeDtypeStruct((m, dv), BF16),
        scratch_shapes=[pltpu.VMEM((MLA_HEADS, 1, t), F32), pltpu.VMEM((MLA_HEADS, MLA_VX, t), F32)],
        compiler_params=_params(("parallel", "arbitrary")),
        name="mla_attn",
    )(qt, k, vt)


def _rot_half_cols(w, heads):
    k = w.shape[0]
    w = w.reshape(k, heads, 2, MLA_ROPE // 2)
    return jnp.stack([-w[:, :, 1], w[:, :, 0]], axis=2).reshape(k, heads * MLA_ROPE)


def kernel(x, positions, ffn1_norm, ffn1_w_gu, ffn1_w_down, mix_norm, ffn2_norm, ffn2_w_gu, ffn2_w_down, gdn_w_in, gdn_conv_w, gdn_a_log, gdn_dt_bias, gdn_out_norm, gdn_w_out, kv_norm, mla_w_kv_a, mla_kv_a_norm, mla_w_kv_b, mla_w_dq, mla_q_norm, mla_w_uq, mla_w_o, final_norm):
    batch, seq, d = x.shape
    m = batch * seq
    depth = ffn1_norm.shape[0]
    n_a = gdn_w_in.shape[0]
    d_qk = GDN_HEADS * GDN_DK
    bf = lambda w: w.astype(BF16)

    h = x.reshape(m, d)
    pos_rows = positions.reshape(m // Q_TILE, 1, Q_TILE)
    half = MLA_ROPE // 2
    inv = ROPE_THETA ** (-jnp.arange(half, dtype=F32) / half)
    inv_col = jnp.tile(inv, LANES // half).reshape(LANES, 1)
    scale = (MLA_NOPE + MLA_ROPE) ** -0.5 * LOG2_E

    w1_gu, w1_down, w2_gu, w2_down = bf(ffn1_w_gu), bf(ffn1_w_down), bf(ffn2_w_gu), bf(ffn2_w_down)
    kk = vt = None
    for layer in range(depth):
        h = _ffn(h, ffn1_norm[layer], w1_gu, w1_down, layer)
        if layer < n_a:
            i = layer
            w_in = gdn_w_in[i]
            wab = jnp.pad(w_in[:, 4 * d_qk:], ((0, 0), (0, LANES - 2 * GDN_HEADS)))
            pad_row = lambda p: jnp.pad(p.astype(F32), (0, LANES - GDN_HEADS)).reshape(1, LANES)
            q, k, v, gate, gb = _gdn_in(
                h, mix_norm[layer], bf(w_in[:, :3 * d_qk]), bf(w_in[:, 3 * d_qk:4 * d_qk]), bf(wab),
                gdn_conv_w[i], pad_row(gdn_a_log[i]), pad_row(gdn_dt_bias[i]), seq)
            mixed = _delta(q, k, v, gate, gb, gdn_out_norm[i], batch, seq)
            w_proj = bf(gdn_w_out[i])
        else:
            j = layer - n_a
            w_uq = mla_w_uq[j].reshape(-1, MLA_HEADS, MLA_NOPE + MLA_ROPE)
            wqn = w_uq[:, :, :MLA_NOPE].reshape(-1, MLA_HEADS * MLA_NOPE)
            wqr = w_uq[:, :, MLA_NOPE:].reshape(-1, MLA_HEADS * MLA_ROPE)
            qt = _mla_q(h, mix_norm[layer], bf(mla_w_dq[j]), mla_q_norm[j], bf(wqn.T), bf(wqr.T),
                        bf(_rot_half_cols(wqr, MLA_HEADS).T), pos_rows, inv_col, scale,
                        batch, seq)
            mixed = _attn(qt, kk, vt, batch, seq)
            w_proj = bf(mla_w_o[j])
        last = layer == depth - 1
        h = _ffn(h, ffn2_norm[layer], w2_gu, w2_down, layer,
                 proj=(mixed, w_proj), final_g=final_norm if last else None)
        if layer == n_a - 1:
            wr = mla_w_kv_a[:, MLA_KV_RANK:]
            w_kv_b = mla_w_kv_b.reshape(MLA_KV_RANK, MLA_HEADS, MLA_NOPE + MLA_V)
            twice = lambda w: jnp.concatenate([w, w], axis=1)
            kk, vt = _mla_kv(
                h, kv_norm, bf(mla_w_kv_a[:, :MLA_KV_RANK]), bf(twice(wr).T),
                bf(twice(_rot_half_cols(wr, 1)).T), mla_kv_a_norm,
                bf(w_kv_b[:, :, :MLA_NOPE].reshape(MLA_KV_RANK, -1)),
                bf(w_kv_b[:, :, MLA_NOPE:].reshape(MLA_KV_RANK, -1).T), pos_rows, inv_col, batch, seq)
    return h.reshape(batch, seq, d)
```
